```python
import jax, jax.numpy as jnp
from jax import lax
import numpy as np

D_MODEL = 4096
BATCH = 4
SEQ = 2048
DEPTH = 1

HEAD_DIM = 128
ATTN_WIDTH = D_MODEL // 2
CONV_WIDTH = D_MODEL - ATTN_WIDTH
N_Q_HEADS = ATTN_WIDTH // HEAD_DIM
N_KV_HEADS = N_Q_HEADS // 4
Q_PER_KV = N_Q_HEADS // N_KV_HEADS
KV_WIDTH = N_KV_HEADS * HEAD_DIM
CONV_GROUP = 128
N_CONV_GROUPS = CONV_WIDTH // CONV_GROUP
CONV_KERNEL = 31
CONV_PAD = CONV_KERNEL // 2
WINDOW = 128
BLOCK = 128
ROPE_THETA = 500000.0
ROPE_DIM = HEAD_DIM // 4
D_FF = -(-8 * D_MODEL // (3 * 256)) * 256
IN_WIDTH = ATTN_WIDTH + 2 * KV_WIDTH + 2 * CONV_WIDTH
RMS_EPS = 1e-6
LN_EPS = 1e-5

kernel_name = "hymba_conformer_swa_sandwich_layer"


def rms_norm(t, g):
    tf = t.astype(jnp.float32)
    y = tf * lax.rsqrt(jnp.mean(tf * tf, axis=-1, keepdims=True) + RMS_EPS)
    return (y * g.astype(jnp.float32)).astype(t.dtype)


def layer_norm(t, g, b):
    tf = t.astype(jnp.float32)
    mu = jnp.mean(tf, axis=-1, keepdims=True)
    var = jnp.mean(jnp.square(tf - mu), axis=-1, keepdims=True)
    y = (tf - mu) * lax.rsqrt(var + LN_EPS)
    return (y * g.astype(jnp.float32) + b.astype(jnp.float32)).astype(t.dtype)


def rope_tables(positions):
    inv_freq = ROPE_THETA ** (-jnp.arange(0, ROPE_DIM, 2, dtype=jnp.float32) / ROPE_DIM)
    ang = positions.astype(jnp.float32)[..., None] * inv_freq
    return jnp.cos(ang)[:, :, None, :], jnp.sin(ang)[:, :, None, :]


def apply_partial_rope(t, cos, sin):
    half = ROPE_DIM // 2
    tf = t[..., :ROPE_DIM].astype(jnp.float32)
    t1, t2 = tf[..., :half], tf[..., half:]
    rot = jnp.concatenate([t1 * cos - t2 * sin, t2 * cos + t1 * sin], axis=-1)
    return jnp.concatenate([rot.astype(t.dtype), t[..., ROPE_DIM:]], axis=-1)


def windowed_gqa_with_sink(q, k, v, sinks):
    B, S = q.shape[0], q.shape[1]
    nb = S // BLOCK
    span = BLOCK + 2 * WINDOW
    qb = q.reshape(B, nb, BLOCK, N_KV_HEADS, Q_PER_KV, HEAD_DIM)
    pad = ((0, 0), (WINDOW, WINDOW), (0, 0), (0, 0))
    kp = jnp.pad(k, pad)
    vp = jnp.pad(v, pad)
    idx = jnp.arange(nb)[:, None] * BLOCK + jnp.arange(span)[None, :]
    kb = kp[:, idx]
    vb = vp[:, idx]
    s = jnp.einsum('bnqhgd,bnkhd->bnhgqk', qb, kb,
                   preferred_element_type=jnp.float32) * (HEAD_DIM ** -0.5)
    qpos = jnp.arange(nb)[:, None] * BLOCK + jnp.arange(BLOCK)[None, :]
    kpos = idx - WINDOW
    rel = kpos[:, None, :] - qpos[:, :, None]
    valid = (jnp.abs(rel) <= WINDOW) & (kpos[:, None, :] >= 0) & (kpos[:, None, :] < S)
    s = jnp.where(valid[None, :, None, None], s, -jnp.inf)
    sink = sinks.astype(jnp.float32).reshape(1, 1, N_KV_HEADS, Q_PER_KV, 1, 1)
    m = jnp.maximum(jnp.max(s, axis=-1, keepdims=True), sink)
    p = jnp.exp(s - m)
    denom = jnp.sum(p, axis=-1, keepdims=True) + jnp.exp(sink - m)
    p = (p / denom).astype(v.dtype)
    o = jnp.einsum('bnhgqk,bnkhd->bnqhgd', p, vb)
    return o.reshape(B, S, N_Q_HEADS * HEAD_DIM)


def conformer_conv(ca, cg, conv_w, conv_b, ln_g, ln_b):
    u = ca * jax.nn.sigmoid(cg)
    y = lax.conv_general_dilated(
        u, conv_w[:, None, :].astype(u.dtype), window_strides=(1,),
        padding=[(CONV_PAD, CONV_PAD)], dimension_numbers=('NWC', 'WIO', 'NWC'),
        feature_group_count=CONV_WIDTH)
    y = y + conv_b.astype(y.dtype)
    y = layer_norm(y, ln_g, ln_b)
    return jax.nn.silu(y)


def setup_inputs(seed: int = 0) -> dict:
    key = jax.random.key(seed)
    ks = jax.random.split(key, 20)
    f32 = jnp.float32

    def gain(k):
        return 1.0 + 0.02 * jax.random.normal(k, (DEPTH, D_MODEL), f32)

    x = jax.random.normal(ks[0], (BATCH, SEQ, D_MODEL), f32)
    positions = jnp.broadcast_to(jnp.arange(SEQ, dtype=jnp.int32), (BATCH, SEQ))
    return {
        "x": x,
        "positions": positions,
        "mix_pre_g": gain(ks[1]),
        "w_in": jax.random.normal(ks[2], (DEPTH, D_MODEL, IN_WIDTH), f32) * D_MODEL ** -0.5,
        "sinks": 0.5 * jax.random.normal(ks[3], (DEPTH, N_Q_HEADS), f32),
        "conv_w": jax.random.normal(ks[4], (DEPTH, CONV_KERNEL, CONV_WIDTH), f32) * CONV_KERNEL ** -0.5,
        "conv_b": 0.02 * jax.random.normal(ks[5], (DEPTH, CONV_WIDTH), f32),
        "conv_ln_g": 1.0 + 0.02 * jax.random.normal(ks[6], (DEPTH, CONV_WIDTH), f32),
        "conv_ln_b": 0.02 * jax.random.normal(ks[7], (DEPTH, CONV_WIDTH), f32),
        "attn_out_g": 1.0 + 0.02 * jax.random.normal(ks[8], (DEPTH, ATTN_WIDTH), f32),
        "conv_out_g": 1.0 + 0.02 * jax.random.normal(ks[9], (DEPTH, CONV_WIDTH), f32),
        "w_out": jax.random.normal(ks[10], (DEPTH, D_MODEL, D_MODEL), f32) * D_MODEL ** -0.5,
        "mix_post_g": gain(ks[11]),
        "ffn_pre_g": gain(ks[12]),
        "w_gate": jax.random.normal(ks[13], (DEPTH, D_MODEL, D_FF), f32) * D_MODEL ** -0.5,
        "w_up": jax.random.normal(ks[14], (DEPTH, D_MODEL, D_FF), f32) * D_MODEL ** -0.5,
        "w_down": jax.random.normal(ks[15], (DEPTH, D_FF, D_MODEL), f32) * D_FF ** -0.5,
        "ffn_post_g": gain(ks[16]),
    }


def reference(x, positions, mix_pre_g, w_in, sinks, conv_w, conv_b, conv_ln_g, conv_ln_b,
              attn_out_g, conv_out_g, w_out, mix_post_g, ffn_pre_g, w_gate, w_up, w_down,
              ffn_post_g):
    B, S = x.shape[0], x.shape[1]
    cos, sin = rope_tables(positions)
    splits = [ATTN_WIDTH, ATTN_WIDTH + KV_WIDTH, ATTN_WIDTH + 2 * KV_WIDTH,
              ATTN_WIDTH + 2 * KV_WIDTH + CONV_WIDTH]
    for l in range(DEPTH):
        h = rms_norm(x, mix_pre_g[l])
        proj = h @ w_in[l]
        q, k, v, ca, cg = jnp.split(proj, splits, axis=-1)
        q = apply_partial_rope(q.reshape(B, S, N_Q_HEADS, HEAD_DIM), cos, sin)
        k = apply_partial_rope(k.reshape(B, S, N_KV_HEADS, HEAD_DIM), cos, sin)
        v = v.reshape(B, S, N_KV_HEADS, HEAD_DIM)
        attn = windowed_gqa_with_sink(q, k, v, sinks[l])
        conv = conformer_conv(ca, cg, conv_w[l], conv_b[l], conv_ln_g[l], conv_ln_b[l])
        merged = jnp.concatenate([rms_norm(attn, attn_out_g[l]),
                                  rms_norm(conv, conv_out_g[l])], axis=-1)
        x = x + rms_norm(merged @ w_out[l], mix_post_g[l])
        h = rms_norm(x, ffn_pre_g[l])
        f = (jax.nn.silu(h @ w_gate[l]) * (h @ w_up[l])) @ w_down[l]
        x = x + rms_norm(f, ffn_post_g[l])
    return x
```

```python
import functools
import math

import jax
import jax.numpy as jnp
from jax import lax
from jax.experimental import pallas as pl
from jax.experimental.pallas import tpu as pltpu

F32 = jnp.float32
BF16 = jnp.bfloat16

HEAD_DIM = 128
ROPE_DIM = HEAD_DIM // 4
ROPE_THETA = 500000.0
WINDOW = 128
RMS_EPS = 1e-6
LN_EPS = 1e-5

V7X_VMEM_BYTES = 64 * 1024 * 1024
VMEM_LIMIT = V7X_VMEM_BYTES - 12 * 1024 * 1024

ROW_TILE = 1024
COL_TILE = 512
NORM_ROWS = 256
Q_TILE = 128
CONV_ROWS = 256
CONV_HALO = 16
FF_ALIGN = 1024


def _params(*sem):
    return pltpu.CompilerParams(dimension_semantics=sem, vmem_limit_bytes=VMEM_LIMIT)


def _rms(t, g):
    ms = jnp.mean(t * t, axis=-1, keepdims=True)
    return t * lax.rsqrt(ms + RMS_EPS) * g


def _prenorm_kernel(x_ref, g_ref, h_ref):
    h_ref[...] = _rms(x_ref[...], g_ref[...]).astype(h_ref.dtype)


def _prenorm(x2, g):
    m, d = x2.shape
    return pl.pallas_call(
        _prenorm_kernel,
        grid=(m // NORM_ROWS,),
        in_specs=[pl.BlockSpec((NORM_ROWS, d), lambda i: (i, 0)),
                  pl.BlockSpec((1, d), lambda i: (0, 0))],
        out_specs=pl.BlockSpec((NORM_ROWS, d), lambda i: (i, 0)),
        out_shape=jax.ShapeDtypeStruct((m, d), BF16),
        compiler_params=_params("parallel"),
        name="prenorm",
    )(x2, g)


def _postmix_kernel(y_ref, x_ref, g1_ref, g2_ref, x1_ref, h2_ref):
    x1 = x_ref[...] + _rms(y_ref[...], g1_ref[...])
    x1_ref[...] = x1
    h2_ref[...] = _rms(x1, g2_ref[...]).astype(h2_ref.dtype)


def _postmix(y, x2, g_post, g_pre):
    m, d = x2.shape
    row = pl.BlockSpec((NORM_ROWS, d), lambda i: (i, 0))
    vec = pl.BlockSpec((1, d), lambda i: (0, 0))
    return pl.pallas_call(
        _postmix_kernel,
        grid=(m // NORM_ROWS,),
        in_specs=[row, row, vec, vec],
        out_specs=[row, row],
        out_shape=[jax.ShapeDtypeStruct((m, d), F32), jax.ShapeDtypeStruct((m, d), BF16)],
        compiler_params=_params("parallel"),
        name="postmix",
    )(y, x2, g_post, g_pre)


def _postffn_kernel(f_ref, x_ref, g_ref, o_ref):
    o_ref[...] = x_ref[...] + _rms(f_ref[...], g_ref[...])


def _postffn(f, x1, g):
    m, d = x1.shape
    row = pl.BlockSpec((NORM_ROWS, d), lambda i: (i, 0))
    return pl.pallas_call(
        _postffn_kernel,
        grid=(m // NORM_ROWS,),
        in_specs=[row, row, pl.BlockSpec((1, d), lambda i: (0, 0))],
        out_specs=row,
        out_shape=jax.ShapeDtypeStruct((m, d), F32),
        compiler_params=_params("parallel"),
        name="postffn",
    )(f, x1, g)


def _qkv_kernel(h_ref, w_ref, pos_ref, o_ref, cos_ref, sup_ref, sdn_ref, *, n_rope_tiles):
    j = pl.program_id(1)
    half = ROPE_DIM // 2

    @pl.when(j == 0)
    def _():
        lane = lax.broadcasted_iota(jnp.int32, (1, HEAD_DIM), 1)
        fidx = (lane % half).astype(F32)
        inv_freq = jnp.exp(fidx * (-2.0 * math.log(ROPE_THETA) / ROPE_DIM))
        ang = pos_ref[...].astype(F32) * inv_freq
        c = jnp.cos(ang)
        s = jnp.sin(ang)
        in_rope = lane < ROPE_DIM
        cos_ref[...] = jnp.where(in_rope, c, 1.0)
        sup_ref[...] = jnp.where(lane < half, -s, 0.0)
        sdn_ref[...] = jnp.where((lane >= half) & in_rope, s, 0.0)

    acc = jnp.dot(h_ref[...], w_ref[...], preferred_element_type=F32)

    @pl.when(j < n_rope_tiles)
    def _():
        for hh in range(acc.shape[1] // HEAD_DIM):
            t = acc[:, hh * HEAD_DIM:(hh + 1) * HEAD_DIM]
            up = pltpu.roll(t, HEAD_DIM - half, axis=1)
            dn = pltpu.roll(t, half, axis=1)
            r = t * cos_ref[...] + up * sup_ref[...] + dn * sdn_ref[...]
            o_ref[:, hh * HEAD_DIM:(hh + 1) * HEAD_DIM] = r.astype(o_ref.dtype)

    @pl.when(j >= n_rope_tiles)
    def _():
        o_ref[...] = acc.astype(o_ref.dtype)


def _qkv_proj(h, w_in_b, pos2, attn_width, kv_width):
    m, d = h.shape
    n = attn_width + 2 * kv_width
    assert attn_width % COL_TILE == 0 and kv_width % COL_TILE == 0 and COL_TILE % HEAD_DIM == 0
    n_rope_tiles = (attn_width + kv_width) // COL_TILE
    return pl.pallas_call(
        functools.partial(_qkv_kernel, n_rope_tiles=n_rope_tiles),
        grid=(m // ROW_TILE, n // COL_TILE),
        in_specs=[pl.BlockSpec((ROW_TILE, d), lambda i, j: (i, 0)),
                  pl.BlockSpec((d, COL_TILE), lambda i, j: (0, j)),
                  pl.BlockSpec((ROW_TILE, 1), lambda i, j: (i, 0))],
        out_specs=pl.BlockSpec((ROW_TILE, COL_TILE), lambda i, j: (i, j)),
        out_shape=jax.ShapeDtypeStruct((m, n), BF16),
        scratch_shapes=[pltpu.VMEM((ROW_TILE, HEAD_DIM), F32)] * 3,
        compiler_params=_params("parallel", "arbitrary"),
        name="qkv_proj",
    )(h, w_in_b, pos2)


def _gated_kernel(h_ref, wa_ref, wb_ref, o_ref, *, mode):
    h = h_ref[...]
    a = jnp.dot(h, wa_ref[...], preferred_element_type=F32)
    b = jnp.dot(h, wb_ref[...], preferred_element_type=F32)
    if mode == "glu":
        r = a * jax.nn.sigmoid(b)
    else:
        r = (a * jax.nn.sigmoid(a)) * b
    o_ref[...] = r.astype(o_ref.dtype)


def _gated_proj(h, wa, wb, a_off, b_off, n_out, mode, out_dtype, name):
    m, d = h.shape
    assert a_off % COL_TILE == 0 and b_off % COL_TILE == 0 and n_out % COL_TILE == 0
    ao, bo = a_off // COL_TILE, b_off // COL_TILE
    return pl.pallas_call(
        functools.partial(_gated_kernel, mode=mode),
        grid=(m // ROW_TILE, n_out // COL_TILE),
        in_specs=[pl.BlockSpec((ROW_TILE, d), lambda i, j: (i, 0)),
                  pl.BlockSpec((d, COL_TILE), lambda i, j: (0, ao + j)),
                  pl.BlockSpec((d, COL_TILE), lambda i, j: (0, bo + j))],
        out_specs=pl.BlockSpec((ROW_TILE, COL_TILE), lambda i, j: (i, j)),
        out_shape=jax.ShapeDtypeStruct((m, n_out), out_dtype),
        compiler_params=_params("parallel", "arbitrary"),
        name=name,
    )(h, wa, wb)


def _attn_kernel(sink_ref, q_ref, k0_ref, k1_ref, k2_ref, v0_ref, v1_ref, v2_ref, g_ref, o_ref, acc_ref,
                 *, n_kv, q_per_kv):
    n = pl.program_id(1)
    nb = pl.num_programs(1)
    span = 3 * Q_TILE
    r = lax.broadcasted_iota(jnp.int32, (Q_TILE, span), 0)
    c = lax.broadcasted_iota(jnp.int32, (Q_TILE, span), 1)
    lo = jnp.where(n == 0, Q_TILE, 0)
    hi = jnp.where(n == nb - 1, 2 * Q_TILE, span)
    rel = c - Q_TILE - r
    valid1 = (rel >= -WINDOW) & (rel <= WINDOW) & (c >= lo) & (c < hi)
    valid = jnp.concatenate([valid1] * q_per_kv, axis=0)
    scale = HEAD_DIM ** -0.5
    for h in range(n_kv):
        hs = slice(h * HEAD_DIM, (h + 1) * HEAD_DIM)
        kh = jnp.concatenate([k0_ref[0, :, hs], k1_ref[0, :, hs], k2_ref[0, :, hs]], axis=0)
        vh = jnp.concatenate([v0_ref[0, :, hs], v1_ref[0, :, hs], v2_ref[0, :, hs]], axis=0)
        qh = jnp.concatenate(
            [q_ref[0, :, (h * q_per_kv + g) * HEAD_DIM:(h * q_per_kv + g + 1) * HEAD_DIM]
             for g in range(q_per_kv)], axis=0)
        s = lax.dot_general(qh, kh, (((1,), (1,)), ((), ())), preferred_element_type=F32) * scale
        s = jnp.where(valid, s, -jnp.inf)
        sink = jnp.concatenate(
            [jnp.full((Q_TILE, 1), sink_ref[h * q_per_kv + g], F32) for g in range(q_per_kv)], axis=0)
        mx = jnp.maximum(jnp.max(s, axis=-1, keepdims=True), sink)
        p = jnp.exp(s - mx)
        denom = jnp.sum(p, axis=-1, keepdims=True) + jnp.exp(sink - mx)
        pn = (p / denom).astype(vh.dtype)
        o = jnp.dot(pn, vh, preferred_element_type=F32)
        for g in range(q_per_kv):
            col = (h * q_per_kv + g) * HEAD_DIM
            acc_ref[:, col:col + HEAD_DIM] = o[g * Q_TILE:(g + 1) * Q_TILE, :]
    o_ref[0] = _rms(acc_ref[...], g_ref[...]).astype(o_ref.dtype)


def _attention(qkv3, sinks, g, attn_width, kv_width):
    b, s, _ = qkv3.shape
    nb = s // Q_TILE
    assert WINDOW == Q_TILE and s % Q_TILE == 0 and attn_width % kv_width == 0
    n_kv = kv_width // HEAD_DIM
    q_per_kv = attn_width // kv_width
    kcol = attn_width // kv_width
    vcol = kcol + 1

    def kv_spec(col, d):
        return pl.BlockSpec((1, Q_TILE, kv_width),
                            lambda bi, n: (bi, jnp.clip(n + d, 0, nb - 1), col))

    return pl.pallas_call(
        functools.partial(_attn_kernel, n_kv=n_kv, q_per_kv=q_per_kv),
        grid=(b, nb),
        in_specs=[pl.BlockSpec(memory_space=pltpu.SMEM),
                  pl.BlockSpec((1, Q_TILE, attn_width), lambda bi, n: (bi, n, 0)),
                  kv_spec(kcol, -1), kv_spec(kcol, 0), kv_spec(kcol, 1),
                  kv_spec(vcol, -1), kv_spec(vcol, 0), kv_spec(vcol, 1),
                  pl.BlockSpec((1, attn_width), lambda bi, n: (0, 0))],
        out_specs=pl.BlockSpec((1, Q_TILE, attn_width), lambda bi, n: (bi, n, 0)),
        out_shape=jax.ShapeDtypeStruct((b, s, attn_width), BF16),
        scratch_shapes=[pltpu.VMEM((Q_TILE, attn_width), F32)],
        compiler_params=_params("parallel", "parallel"),
        name="swa_attention",
    )(sinks, qkv3, qkv3, qkv3, qkv3, qkv3, qkv3, qkv3, g)


CONV_RC = 64
CONV_CC = 256
LN_ROWS = 16


def _conv_kernel(up_ref, u_ref, un_ref, w_ref, cb_ref, lg_ref, lb_ref, og_ref, o_ref, slab_ref, y_ref,
                 *, ksize):
    t = pl.program_id(1)
    nt = pl.num_programs(1)
    rows, ch = y_ref.shape
    halo = CONV_HALO
    slab_ref[0:halo, :] = jnp.where(t == 0, 0.0, up_ref[0])
    slab_ref[halo:halo + rows, :] = u_ref[0]
    slab_ref[halo + rows:halo + rows + halo, :] = jnp.where(t == nt - 1, 0.0, un_ref[0])
    off = halo - ksize // 2
    n_shift = 8
    ext = ((off + ksize - 1) // n_shift) * n_shift

    def col_chunk(ci, carry):
        c0 = pl.multiple_of(ci * CONV_CC, CONV_CC)
        cs = pl.ds(c0, CONV_CC)
        w = w_ref[:, cs]
        bias = cb_ref[:, cs]
        for r0 in range(0, rows, CONV_RC):
            acc = jnp.zeros((CONV_RC, CONV_CC), F32)
            for sft in range(n_shift):
                blk = slab_ref[pl.ds(r0 + sft, CONV_RC + ext), cs]
                for q in range(ext // n_shift + 1):
                    k = q * n_shift + sft - off
                    if 0 <= k < ksize:
                        acc = acc + blk[q * n_shift:q * n_shift + CONV_RC, :] * w[k:k + 1, :]
            y_ref[pl.ds(r0, CONV_RC), cs] = acc + bias
        return carry

    lax.fori_loop(0, ch // CONV_CC, col_chunk, 0)

    def row_chunk(ri, carry):
        rs = pl.ds(pl.multiple_of(ri * LN_ROWS, LN_ROWS), LN_ROWS)
        y = y_ref[rs, :]
        mu = jnp.mean(y, axis=-1, keepdims=True)
        d = y - mu
        var = jnp.mean(d * d, axis=-1, keepdims=True)
        z = d * lax.rsqrt(var + LN_EPS) * lg_ref[...] + lb_ref[...]
        sw = z * jax.nn.sigmoid(z)
        o_ref[0, rs, :] = _rms(sw, og_ref[...]).astype(o_ref.dtype)
        return carry

    lax.fori_loop(0, rows // LN_ROWS, row_chunk, 0)


def _conv_branch(u3, conv_w, conv_b, ln_g, ln_b, out_g):
    b, s, ch = u3.shape
    ksize = conv_w.shape[0]
    assert ksize // 2 <= CONV_HALO and s % CONV_ROWS == 0 and ch % CONV_CC == 0
    hb = CONV_ROWS // CONV_HALO
    last = s // CONV_HALO - 1
    vec = pl.BlockSpec((1, ch), lambda bi, t: (0, 0))
    return pl.pallas_call(
        functools.partial(_conv_kernel, ksize=ksize),
        grid=(b, s // CONV_ROWS),
        in_specs=[pl.BlockSpec((1, CONV_HALO, ch), lambda bi, t: (bi, jnp.maximum(t * hb - 1, 0), 0)),
                  pl.BlockSpec((1, CONV_ROWS, ch), lambda bi, t: (bi, t, 0)),
                  pl.BlockSpec((1, CONV_HALO, ch), lambda bi, t: (bi, jnp.minimum((t + 1) * hb, last), 0)),
                  pl.BlockSpec((ksize, ch), lambda bi, t: (0, 0)),
                  vec, vec, vec, vec],
        out_specs=pl.BlockSpec((1, CONV_ROWS, ch), lambda bi, t: (bi, t, 0)),
        out_shape=jax.ShapeDtypeStruct((b, s, ch), BF16),
        scratch_shapes=[pltpu.VMEM((CONV_ROWS + 2 * CONV_HALO, ch), F32),
                        pltpu.VMEM((CONV_ROWS, ch), F32)],
        compiler_params=_params("parallel", "parallel"),
        name="conv_branch",
    )(u3, u3, u3, conv_w, conv_b, ln_g, ln_b, out_g)


def _outproj_kernel(a1_ref, a2_ref, w1_ref, w2_ref, o_ref):
    o_ref[...] = (jnp.dot(a1_ref[...], w1_ref[...], preferred_element_type=F32)
                  + jnp.dot(a2_ref[...], w2_ref[...], preferred_element_type=F32))


def _outproj(a1, a2, w_b):
    m, k1 = a1.shape
    k2 = a2.shape[1]
    n = w_b.shape[1]
    assert k1 == k2 and w_b.shape[0] == k1 + k2
    bn = 2 * COL_TILE
    return pl.pallas_call(
        _outproj_kernel,
        grid=(m // ROW_TILE, n // bn),
        in_specs=[pl.BlockSpec((ROW_TILE, k1), lambda i, j: (i, 0)),
                  pl.BlockSpec((ROW_TILE, k2), lambda i, j: (i, 0)),
                  pl.BlockSpec((k1, bn), lambda i, j: (0, j)),
                  pl.BlockSpec((k2, bn), lambda i, j: (1, j))],
        out_specs=pl.BlockSpec((ROW_TILE, bn), lambda i, j: (i, j)),
        out_shape=jax.ShapeDtypeStruct((m, n), F32),
        compiler_params=_params("parallel", "arbitrary"),
        name="outproj",
    )(a1, a2, w_b, w_b)


def _down_kernel(a_ref, w_ref, o_ref):
    k = pl.program_id(2)
    d = jnp.dot(a_ref[...], w_ref[...], preferred_element_type=F32)

    @pl.when(k == 0)
    def _():
        o_ref[...] = d

    @pl.when(k > 0)
    def _():
        o_ref[...] += d


def _down_proj(a, w_b):
    m, kdim = a.shape
    n = w_b.shape[1]
    bn = 2 * ROW_TILE
    bk = FF_ALIGN
    assert kdim % bk == 0 and n % bn == 0
    return pl.pallas_call(
        _down_kernel,
        grid=(m // ROW_TILE, n // bn, kdim // bk),
        in_specs=[pl.BlockSpec((ROW_TILE, bk), lambda i, j, k: (i, k)),
                  pl.BlockSpec((bk, bn), lambda i, j, k: (k, j))],
        out_specs=pl.BlockSpec((ROW_TILE, bn), lambda i, j, k: (i, j)),
        out_shape=jax.ShapeDtypeStruct((m, n), F32),
        compiler_params=_params("parallel", "parallel", "arbitrary"),
        name="down_proj",
    )(a, w_b)


def kernel(x, positions, mix_pre_g, w_in, sinks, conv_w, conv_b, conv_ln_g, conv_ln_b, attn_out_g,
           conv_out_g, w_out, mix_post_g, ffn_pre_g, w_gate, w_up, w_down, ffn_post_g):
    b, s, d = x.shape
    depth = w_in.shape[0]
    attn_width = attn_out_g.shape[1]
    conv_width = conv_out_g.shape[1]
    kv_width = (w_in.shape[2] - attn_width - 2 * conv_width) // 2
    d_ff = w_gate.shape[2]
    ff_pad = -(-d_ff // FF_ALIGN) * FF_ALIGN
    m = b * s
    assert m % ROW_TILE == 0 and m % NORM_ROWS == 0

    x2 = x.reshape(m, d)
    pos2 = positions.reshape(m, 1)
    row = lambda v: v.reshape(1, -1)

    for l in range(depth):
        w_in_b = w_in[l].astype(BF16)
        w_out_b = w_out[l].astype(BF16)
        w_gate_b = jnp.pad(w_gate[l].astype(BF16), ((0, 0), (0, ff_pad - d_ff)))
        w_up_b = jnp.pad(w_up[l].astype(BF16), ((0, 0), (0, ff_pad - d_ff)))
        w_down_b = jnp.pad(w_down[l].astype(BF16), ((0, ff_pad - d_ff), (0, 0)))

        h1 = _prenorm(x2, row(mix_pre_g[l]))
        qkv = _qkv_proj(h1, w_in_b, pos2, attn_width, kv_width)
        qkv_n = attn_width + 2 * kv_width
        u = _gated_proj(h1, w_in_b, w_in_b, qkv_n, qkv_n + conv_width, conv_width, "glu", F32, "conv_in_proj")

        attn_n = _attention(qkv.reshape(b, s, qkv_n), sinks[l], row(attn_out_g[l]), attn_width, kv_width)
        conv_n = _conv_branch(u.reshape(b, s, conv_width), conv_w[l], row(conv_b[l]), row(conv_ln_g[l]),
                              row(conv_ln_b[l]), row(conv_out_g[l]))

        y = _outproj(attn_n.reshape(m, attn_width), conv_n.reshape(m, conv_width), w_out_b)
        x1, h2 = _postmix(y, x2, row(mix_post_g[l]), row(ffn_pre_g[l]))

        a = _gated_proj(h2, w_gate_b, w_up_b, 0, 0, ff_pad, "swiglu", BF16, "ffn_up")
        f = _down_proj(a, w_down_b)
        x2 = _postffn(f, x1, row(ffn_post_g[l]))
    return x2.reshape(b, s, d)
```

```python
import functools
import math

import jax
import jax.numpy as jnp
from jax import lax
from jax.experimental import pallas as pl
from jax.experimental.pallas import tpu as pltpu

F32 = jnp.float32
BF16 = jnp.bfloat16

HEAD_DIM = 128
ROPE_DIM = HEAD_DIM // 4
ROPE_THETA = 500000.0
WINDOW = 128
RMS_EPS = 1e-6
LN_EPS = 1e-5

V7X_VMEM_BYTES = 64 * 1024 * 1024
VMEM_LIMIT = V7X_VMEM_BYTES - 12 * 1024 * 1024

ROW_TILE = 1024
COL_TILE = 512
NORM_ROWS = 256
Q_TILE = 128
CONV_ROWS = 256
CONV_HALO = 16


def _params(*sem):
    return pltpu.CompilerParams(dimension_semantics=sem, vmem_limit_bytes=VMEM_LIMIT)


def _rms(t, g):
    ms = jnp.mean(t * t, axis=-1, keepdims=True)
    return t * lax.rsqrt(ms + RMS_EPS) * g


def _prenorm_kernel(x_ref, g_ref, h_ref):
    h_ref[...] = _rms(x_ref[...], g_ref[...]).astype(h_ref.dtype)


def _prenorm(x2, g):
    m, d = x2.shape
    return pl.pallas_call(
        _prenorm_kernel,
        grid=(m // NORM_ROWS,),
        in_specs=[pl.BlockSpec((NORM_ROWS, d), lambda i: (i, 0)),
                  pl.BlockSpec((1, d), lambda i: (0, 0))],
        out_specs=pl.BlockSpec((NORM_ROWS, d), lambda i: (i, 0)),
        out_shape=jax.ShapeDtypeStruct((m, d), BF16),
        compiler_params=_params("parallel"),
        name="prenorm",
    )(x2, g)


def _postmix_kernel(y_ref, x_ref, g1_ref, g2_ref, x1_ref, h2_ref):
    x1 = x_ref[...] + _rms(y_ref[...], g1_ref[...])
    x1_ref[...] = x1
    h2_ref[...] = _rms(x1, g2_ref[...]).astype(h2_ref.dtype)


def _postmix(y, x2, g_post, g_pre):
    m, d = x2.shape
    row = pl.BlockSpec((NORM_ROWS, d), lambda i: (i, 0))
    vec = pl.BlockSpec((1, d), lambda i: (0, 0))
    return pl.pallas_call(
        _postmix_kernel,
        grid=(m // NORM_ROWS,),
        in_specs=[row, row, vec, vec],
        out_specs=[row, row],
        out_shape=[jax.ShapeDtypeStruct((m, d), F32), jax.ShapeDtypeStruct((m, d), BF16)],
        compiler_params=_params("parallel"),
        name="postmix",
    )(y, x2, g_post, g_pre)


def _qkv_kernel(h_ref, w_ref, pos_ref, o_ref, cos_ref, sup_ref, sdn_ref, *, n_rope_tiles):
    j = pl.program_id(1)
    half = ROPE_DIM // 2

    @pl.when(j == 0)
    def _():
        lane = lax.broadcasted_iota(jnp.int32, (1, HEAD_DIM), 1)
        fidx = (lane % half).astype(F32)
        inv_freq = jnp.exp(fidx * (-2.0 * math.log(ROPE_THETA) / ROPE_DIM))
        ang = pos_ref[...].astype(F32) * inv_freq
        c = jnp.cos(ang)
        s = jnp.sin(ang)
        in_rope = lane < ROPE_DIM
        cos_ref[...] = jnp.where(in_rope, c, 1.0)
        sup_ref[...] = jnp.where(lane < half, -s, 0.0)
        sdn_ref[...] = jnp.where((lane >= half) & in_rope, s, 0.0)

    acc = jnp.dot(h_ref[...], w_ref[...], preferred_element_type=F32)

    @pl.when(j < n_rope_tiles)
    def _():
        for hh in range(acc.shape[1] // HEAD_DIM):
            t = acc[:, hh * HEAD_DIM:(hh + 1) * HEAD_DIM]
            up = pltpu.roll(t, HEAD_DIM - half, axis=1)
            dn = pltpu.roll(t, half, axis=1)
            r = t * cos_ref[...] + up * sup_ref[...] + dn * sdn_ref[...]
            o_ref[:, hh * HEAD_DIM:(hh + 1) * HEAD_DIM] = r.astype(o_ref.dtype)

    @pl.when(j >= n_rope_tiles)
    def _():
        o_ref[...] = acc.astype(o_ref.dtype)


def _qkv_proj(h, w_in_b, pos2, attn_width, kv_width):
    m, d = h.shape
    n = attn_width + 2 * kv_width
    assert attn_width % COL_TILE == 0 and kv_width % COL_TILE == 0 and COL_TILE % HEAD_DIM == 0
    n_rope_tiles = (attn_width + kv_width) // COL_TILE
    return pl.pallas_call(
        functools.partial(_qkv_kernel, n_rope_tiles=n_rope_tiles),
        grid=(m // ROW_TILE, n // COL_TILE),
        in_specs=[pl.BlockSpec((ROW_TILE, d), lambda i, j: (i, 0)),
                  pl.BlockSpec((d, COL_TILE), lambda i, j: (0, j)),
                  pl.BlockSpec((ROW_TILE, 1), lambda i, j: (i, 0))],
        out_specs=pl.BlockSpec((ROW_TILE, COL_TILE), lambda i, j: (i, j)),
        out_shape=jax.ShapeDtypeStruct((m, n), BF16),
        scratch_shapes=[pltpu.VMEM((ROW_TILE, HEAD_DIM), F32)] * 3,
        compiler_params=_params("parallel", "arbitrary"),
        name="qkv_proj",
    )(h, w_in_b, pos2)


def _glu_kernel(h_ref, wa_ref, wb_ref, o_ref):
    h = h_ref[...]
    a = jnp.dot(h, wa_ref[...], preferred_element_type=F32)
    b = jnp.dot(h, wb_ref[...], preferred_element_type=F32)
    o_ref[...] = (a * jax.nn.sigmoid(b)).astype(o_ref.dtype)


def _glu_proj(h, w, a_off, b_off, n_out):
    m, d = h.shape
    assert a_off % COL_TILE == 0 and b_off % COL_TILE == 0 and n_out % COL_TILE == 0
    ao, bo = a_off // COL_TILE, b_off // COL_TILE
    return pl.pallas_call(
        _glu_kernel,
        grid=(m // ROW_TILE, n_out // COL_TILE),
        in_specs=[pl.BlockSpec((ROW_TILE, d), lambda i, j: (i, 0)),
                  pl.BlockSpec((d, COL_TILE), lambda i, j: (0, ao + j)),
                  pl.BlockSpec((d, COL_TILE), lambda i, j: (0, bo + j))],
        out_specs=pl.BlockSpec((ROW_TILE, COL_TILE), lambda i, j: (i, j)),
        out_shape=jax.ShapeDtypeStruct((m, n_out), F32),
        compiler_params=_params("parallel", "arbitrary"),
        name="conv_in_proj",
    )(h, w, w)


def _attn_kernel(sink_ref, q_ref, k0_ref, k1_ref, k2_ref, v0_ref, v1_ref, v2_ref, g_ref, o_ref, acc_ref,
                 *, n_kv, q_per_kv):
    n = pl.program_id(1)
    nb = pl.num_programs(1)
    span = 3 * Q_TILE
    r = lax.broadcasted_iota(jnp.int32, (Q_TILE, span), 0)
    c = lax.broadcasted_iota(jnp.int32, (Q_TILE, span), 1)
    lo = jnp.where(n == 0, Q_TILE, 0)
    hi = jnp.where(n == nb - 1, 2 * Q_TILE, span)
    rel = c - Q_TILE - r
    valid1 = (rel >= -WINDOW) & (rel <= WINDOW) & (c >= lo) & (c < hi)
    valid = jnp.concatenate([valid1] * q_per_kv, axis=0)
    scale = HEAD_DIM ** -0.5
    for h in range(n_kv):
        hs = slice(h * HEAD_DIM, (h + 1) * HEAD_DIM)
        kh = jnp.concatenate([k0_ref[0, :, hs], k1_ref[0, :, hs], k2_ref[0, :, hs]], axis=0)
        vh = jnp.concatenate([v0_ref[0, :, hs], v1_ref[0, :, hs], v2_ref[0, :, hs]], axis=0)
        qh = jnp.concatenate(
            [q_ref[0, :, (h * q_per_kv + g) * HEAD_DIM:(h * q_per_kv + g + 1) * HEAD_DIM]
             for g in range(q_per_kv)], axis=0)
        s = lax.dot_general(qh, kh, (((1,), (1,)), ((), ())), preferred_element_type=F32) * scale
        s = jnp.where(valid, s, -jnp.inf)
        sink = jnp.concatenate(
            [jnp.full((Q_TILE, 1), sink_ref[h * q_per_kv + g], F32) for g in range(q_per_kv)], axis=0)
        mx = jnp.maximum(jnp.max(s, axis=-1, keepdims=True), sink)
        p = jnp.exp(s - mx)
        denom = jnp.sum(p, axis=-1, keepdims=True) + jnp.exp(sink - mx)
        pn = (p / denom).astype(vh.dtype)
        o = jnp.dot(pn, vh, preferred_element_type=F32)
        for g in range(q_per_kv):
            col = (h * q_per_kv + g) * HEAD_DIM
            acc_ref[:, col:col + HEAD_DIM] = o[g * Q_TILE:(g + 1) * Q_TILE, :]
    o_ref[0] = _rms(acc_ref[...], g_ref[...]).astype(o_ref.dtype)


def _attention(qkv3, sinks, g, attn_width, kv_width):
    b, s, _ = qkv3.shape
    nb = s // Q_TILE
    assert WINDOW == Q_TILE and s % Q_TILE == 0 and attn_width % kv_width == 0
    n_kv = kv_width // HEAD_DIM
    q_per_kv = attn_width // kv_width
    kcol = attn_width // kv_width
    vcol = kcol + 1

    def kv_spec(col, d):
        return pl.BlockSpec((1, Q_TILE, kv_width),
                            lambda bi, n: (bi, jnp.clip(n + d, 0, nb - 1), col))

    return pl.pallas_call(
        functools.partial(_attn_kernel, n_kv=n_kv, q_per_kv=q_per_kv),
        grid=(b, nb),
        in_specs=[pl.BlockSpec(memory_space=pltpu.SMEM),
                  pl.BlockSpec((1, Q_TILE, attn_width), lambda bi, n: (bi, n, 0)),
                  kv_spec(kcol, -1), kv_spec(kcol, 0), kv_spec(kcol, 1),
                  kv_spec(vcol, -1), kv_spec(vcol, 0), kv_spec(vcol, 1),
                  pl.BlockSpec((1, attn_width), lambda bi, n: (0, 0))],
        out_specs=pl.BlockSpec((1, Q_TILE, attn_width), lambda bi, n: (bi, n, 0)),
        out_shape=jax.ShapeDtypeStruct((b, s, attn_width), BF16),
        scratch_shapes=[pltpu.VMEM((Q_TILE, attn_width), F32)],
        compiler_params=_params("parallel", "parallel"),
        name="swa_attention",
    )(sinks, qkv3, qkv3, qkv3, qkv3, qkv3, qkv3, qkv3, g)


CONV_RC = 64
CONV_CC = 128
SUBLANES = 8


def _conv_kernel(up_ref, u_ref, un_ref, wb_ref, cb_ref, lg_ref, lb_ref, og_ref, o_ref, slab_ref, y_ref,
                 *, ksize):
    t = pl.program_id(1)
    nt = pl.num_programs(1)
    rows, ch = y_ref.shape
    halo = CONV_HALO
    slab_ref[0:halo, :] = jnp.where(t == 0, 0.0, up_ref[0])
    slab_ref[halo:halo + rows, :] = u_ref[0]
    slab_ref[halo + rows:halo + rows + halo, :] = jnp.where(t == nt - 1, 0.0, un_ref[0])
    off = halo - ksize // 2
    span = CONV_RC + 2 * halo
    n_rc = rows // CONV_RC

    def chunk(idx, carry):
        ci = idx // n_rc
        ri = idx - ci * n_rc
        cs = pl.ds(pl.multiple_of(ci * CONV_CC, CONV_CC), CONV_CC)
        r0 = pl.multiple_of(ri * CONV_RC, CONV_RC)
        raw = slab_ref[pl.ds(r0, span), cs]
        acc = jnp.zeros((CONV_RC, CONV_CC), F32)
        for sft in range(SUBLANES):
            sh = raw if sft == 0 else pltpu.roll(raw, span - sft, axis=0)
            for q in range(span // SUBLANES):
                k = q * SUBLANES + sft - off
                if 0 <= k < ksize:
                    wk = jnp.concatenate([wb_ref[k, :, cs]] * (CONV_RC // SUBLANES), axis=0)
                    acc = acc + sh[q * SUBLANES:q * SUBLANES + CONV_RC, :] * wk
        y_ref[pl.ds(r0, CONV_RC), cs] = acc + cb_ref[:, cs]
        return carry

    lax.fori_loop(0, (ch // CONV_CC) * n_rc, chunk, 0, unroll=2)

    y = y_ref[...]
    mu = jnp.mean(y, axis=-1, keepdims=True)
    d = y - mu
    var = jnp.mean(d * d, axis=-1, keepdims=True)
    z = d * lax.rsqrt(var + LN_EPS) * lg_ref[...] + lb_ref[...]
    sw = z * jax.nn.sigmoid(z)
    o_ref[0] = _rms(sw, og_ref[...]).astype(o_ref.dtype)


def _conv_branch(u3, conv_w, conv_b, ln_g, ln_b, out_g):
    b, s, ch = u3.shape
    ksize = conv_w.shape[0]
    assert ksize // 2 <= CONV_HALO and s % CONV_ROWS == 0 and ch % CONV_CC == 0
    w_bcast = jnp.broadcast_to(conv_w[:, None, :], (ksize, SUBLANES, ch))
    hb = CONV_ROWS // CONV_HALO
    last = s // CONV_HALO - 1
    vec = pl.BlockSpec((1, ch), lambda bi, t: (0, 0))
    return pl.pallas_call(
        functools.partial(_conv_kernel, ksize=ksize),
        grid=(b, s // CONV_ROWS),
        in_specs=[pl.BlockSpec((1, CONV_HALO, ch), lambda bi, t: (bi, jnp.maximum(t * hb - 1, 0), 0)),
                  pl.BlockSpec((1, CONV_ROWS, ch), lambda bi, t: (bi, t, 0)),
                  pl.BlockSpec((1, CONV_HALO, ch), lambda bi, t: (bi, jnp.minimum((t + 1) * hb, last), 0)),
                  pl.BlockSpec((ksize, SUBLANES, ch), lambda bi, t: (0, 0, 0)),
                  vec, vec, vec, vec],
        out_specs=pl.BlockSpec((1, CONV_ROWS, ch), lambda bi, t: (bi, t, 0)),
        out_shape=jax.ShapeDtypeStruct((b, s, ch), BF16),
        scratch_shapes=[pltpu.VMEM((CONV_ROWS + 2 * CONV_HALO, ch), F32),
                        pltpu.VMEM((CONV_ROWS, ch), F32)],
        compiler_params=_params("parallel", "parallel"),
        name="conv_branch",
    )(u3, u3, u3, w_bcast, conv_b, ln_g, ln_b, out_g)


def _outproj_kernel(a1_ref, a2_ref, w1_ref, w2_ref, o_ref):
    o_ref[...] = (jnp.dot(a1_ref[...], w1_ref[...], preferred_element_type=F32)
                  + jnp.dot(a2_ref[...], w2_ref[...], preferred_element_type=F32))


def _outproj(a1, a2, w_b):
    m, k1 = a1.shape
    k2 = a2.shape[1]
    n = w_b.shape[1]
    assert k1 == k2 and w_b.shape[0] == k1 + k2
    bn = 2 * COL_TILE
    return pl.pallas_call(
        _outproj_kernel,
        grid=(m // ROW_TILE, n // bn),
        in_specs=[pl.BlockSpec((ROW_TILE, k1), lambda i, j: (i, 0)),
                  pl.BlockSpec((ROW_TILE, k2), lambda i, j: (i, 0)),
                  pl.BlockSpec((k1, bn), lambda i, j: (0, j)),
                  pl.BlockSpec((k2, bn), lambda i, j: (1, j))],
        out_specs=pl.BlockSpec((ROW_TILE, bn), lambda i, j: (i, j)),
        out_shape=jax.ShapeDtypeStruct((m, n), F32),
        compiler_params=_params("parallel", "arbitrary"),
        name="outproj",
    )(a1, a2, w_b, w_b)


FF_TILE = 256
FIN_TILE = 512
DOWN_CHUNK = 512


def _ffn_kernel(h_ref, wg_ref, wu_ref, wd_ref, x1_ref, g_ref, o_ref, acc_ref, scale_ref, *, n_f):
    j = pl.program_id(1)
    d_model = acc_ref.shape[1]

    @pl.when(j == 0)
    def _():
        acc_ref[...] = jnp.zeros_like(acc_ref)

    @pl.when(j < n_f)
    def _():
        h = h_ref[...]
        gate = jnp.dot(h, wg_ref[...], preferred_element_type=F32)
        up = jnp.dot(h, wu_ref[...], preferred_element_type=F32)
        a = ((gate * jax.nn.sigmoid(gate)) * up).astype(h.dtype)
        for n0 in range(0, d_model, DOWN_CHUNK):
            acc_ref[:, n0:n0 + DOWN_CHUNK] += jnp.dot(a, wd_ref[:, n0:n0 + DOWN_CHUNK],
                                                       preferred_element_type=F32)

    @pl.when(j == n_f)
    def _():
        y = acc_ref[...]
        scale_ref[...] = lax.rsqrt(jnp.mean(y * y, axis=-1, keepdims=True) + RMS_EPS)

    @pl.when(j >= n_f)
    def _():
        cs = pl.ds(pl.multiple_of((j - n_f) * FIN_TILE, FIN_TILE), FIN_TILE)
        o_ref[...] = x1_ref[...] + acc_ref[:, cs] * scale_ref[...] * g_ref[...]


def _ffn(h, w_gate_b, w_up_b, w_down_b, x1, g):
    m, d = h.shape
    d_ff = w_gate_b.shape[1]
    assert d_ff % FF_TILE == 0 and d % FIN_TILE == 0 and d % DOWN_CHUNK == 0
    n_f = d_ff // FF_TILE
    n_fin = d // FIN_TILE
    ftile = lambda i, j: jnp.minimum(j, n_f - 1)
    fin = lambda i, j: jnp.maximum(j - n_f, 0)
    return pl.pallas_call(
        functools.partial(_ffn_kernel, n_f=n_f),
        grid=(m // ROW_TILE, n_f + n_fin),
        in_specs=[pl.BlockSpec((ROW_TILE, d), lambda i, j: (i, 0), pipeline_mode=pl.Buffered(1)),
                  pl.BlockSpec((d, FF_TILE), lambda i, j: (0, ftile(i, j))),
                  pl.BlockSpec((d, FF_TILE), lambda i, j: (0, ftile(i, j))),
                  pl.BlockSpec((FF_TILE, d), lambda i, j: (ftile(i, j), 0)),
                  pl.BlockSpec((ROW_TILE, FIN_TILE), lambda i, j: (i, fin(i, j))),
                  pl.BlockSpec((1, FIN_TILE), lambda i, j: (0, fin(i, j)))],
        out_specs=pl.BlockSpec((ROW_TILE, FIN_TILE), lambda i, j: (i, fin(i, j))),
        out_shape=jax.ShapeDtypeStruct((m, d), F32),
        scratch_shapes=[pltpu.VMEM((ROW_TILE, d), F32), pltpu.VMEM((ROW_TILE, 1), F32)],
        compiler_params=_params("parallel", "arbitrary"),
        name="ffn",
    )(h, w_gate_b, w_up_b, w_down_b, x1, g)


def kernel(x, positions, mix_pre_g, w_in, sinks, conv_w, conv_b, conv_ln_g, conv_ln_b, attn_out_g,
           conv_out_g, w_out, mix_post_g, ffn_pre_g, w_gate, w_up, w_down, ffn_post_g):
    b, s, d = x.shape
    depth = w_in.shape[0]
    attn_width = attn_out_g.shape[1]
    conv_width = conv_out_g.shape[1]
    kv_width = (w_in.shape[2] - attn_width - 2 * conv_width) // 2
    m = b * s
    assert m % ROW_TILE == 0 and m % NORM_ROWS == 0

    x2 = x.reshape(m, d)
    pos2 = positions.reshape(m, 1)
    row = lambda v: v.reshape(1, -1)

    for l in range(depth):
        w_in_b = w_in[l].astype(BF16)
        w_out_b = w_out[l].astype(BF16)
        w_gate_b = w_gate[l].astype(BF16)
        w_up_b = w_up[l].astype(BF16)
        w_down_b = w_down[l].astype(BF16)

        h1 = _prenorm(x2, row(mix_pre_g[l]))
        qkv = _qkv_proj(h1, w_in_b, pos2, attn_width, kv_width)
        qkv_n = attn_width + 2 * kv_width
        u = _glu_proj(h1, w_in_b, qkv_n, qkv_n + conv_width, conv_width)

        attn_n = _attention(qkv.reshape(b, s, qkv_n), sinks[l], row(attn_out_g[l]), attn_width, kv_width)
        conv_n = _conv_branch(u.reshape(b, s, conv_width), conv_w[l], row(conv_b[l]), row(conv_ln_g[l]),
                              row(conv_ln_b[l]), row(conv_out_g[l]))

        y = _outproj(attn_n.reshape(m, attn_width), conv_n.reshape(m, conv_width), w_out_b)
        x1, h2 = _postmix(y, x2, row(mix_post_g[l]), row(ffn_pre_g[l]))

        x2 = _ffn(h2, w_gate_b, w_up_b, w_down_b, x1, row(ffn_post_g[l]))
    return x2.reshape(b, s, d)
```

```python
import functools
import math

import jax
import jax.numpy as jnp
from jax import lax
from jax.experimental import pallas as pl
from jax.experimental.pallas import tpu as pltpu

F32 = jnp.float32
BF16 = jnp.bfloat16

HEAD_DIM = 128
ROPE_DIM = HEAD_DIM // 4
ROPE_THETA = 500000.0
WINDOW = 128
RMS_EPS = 1e-6
LN_EPS = 1e-5

V7X_VMEM_BYTES = 64 * 1024 * 1024
VMEM_LIMIT = V7X_VMEM_BYTES - 12 * 1024 * 1024

ROW_TILE = 1024
COL_TILE = 512
NORM_ROWS = 256
Q_TILE = 128
CONV_ROWS = 256
CONV_HALO = 16


def _params(*sem):
    return pltpu.CompilerParams(dimension_semantics=sem, vmem_limit_bytes=VMEM_LIMIT)


def _rms(t, g):
    ms = jnp.mean(t * t, axis=-1, keepdims=True)
    return t * lax.rsqrt(ms + RMS_EPS) * g


BF16_ROWS = 16


def _cast_spec(w, n_steps, step):
    r, c = w.shape
    rows = next(d for d in range(BF16_ROWS, r + 1, BF16_ROWS) if r % d == 0 and d * n_steps >= r)
    last = r // rows - 1
    spec = pl.BlockSpec((rows, c), lambda *g: (jnp.minimum(step(*g), last), 0))
    return spec, jax.ShapeDtypeStruct((r, c), BF16)


def _prenorm_kernel(x_ref, g_ref, h_ref):
    h_ref[...] = _rms(x_ref[...], g_ref[...]).astype(h_ref.dtype)


def _prenorm(x2, g):
    m, d = x2.shape
    return pl.pallas_call(
        _prenorm_kernel,
        grid=(m // NORM_ROWS,),
        in_specs=[pl.BlockSpec((NORM_ROWS, d), lambda i: (i, 0)),
                  pl.BlockSpec((1, d), lambda i: (0, 0))],
        out_specs=pl.BlockSpec((NORM_ROWS, d), lambda i: (i, 0)),
        out_shape=jax.ShapeDtypeStruct((m, d), BF16),
        compiler_params=_params("parallel"),
        name="prenorm",
    )(x2, g)


def _postmix_kernel(y_ref, x_ref, g1_ref, g2_ref, x1_ref, h2_ref):
    x1 = x_ref[...] + _rms(y_ref[...], g1_ref[...])
    x1_ref[...] = x1
    h2_ref[...] = _rms(x1, g2_ref[...]).astype(h2_ref.dtype)


def _postmix(y, x2, g_post, g_pre):
    m, d = x2.shape
    row = pl.BlockSpec((NORM_ROWS, d), lambda i: (i, 0))
    vec = pl.BlockSpec((1, d), lambda i: (0, 0))
    return pl.pallas_call(
        _postmix_kernel,
        grid=(m // NORM_ROWS,),
        in_specs=[row, row, vec, vec],
        out_specs=[row, row],
        out_shape=[jax.ShapeDtypeStruct((m, d), F32), jax.ShapeDtypeStruct((m, d), BF16)],
        compiler_params=_params("parallel"),
        name="postmix",
    )(y, x2, g_post, g_pre)


def _qkv_kernel(h_ref, w_ref, pos_ref, wsrc_ref, o_ref, wdst_ref, cos_ref, sup_ref, sdn_ref, *, n_rope_tiles):
    j = pl.program_id(1)
    half = ROPE_DIM // 2
    wdst_ref[...] = wsrc_ref[...].astype(wdst_ref.dtype)

    @pl.when(j == 0)
    def _():
        lane = lax.broadcasted_iota(jnp.int32, (1, HEAD_DIM), 1)
        fidx = (lane % half).astype(F32)
        inv_freq = jnp.exp(fidx * (-2.0 * math.log(ROPE_THETA) / ROPE_DIM))
        ang = pos_ref[...].astype(F32) * inv_freq
        c = jnp.cos(ang)
        s = jnp.sin(ang)
        in_rope = lane < ROPE_DIM
        cos_ref[...] = jnp.where(in_rope, c, 1.0)
        sup_ref[...] = jnp.where(lane < half, -s, 0.0)
        sdn_ref[...] = jnp.where((lane >= half) & in_rope, s, 0.0)

    acc = jnp.dot(h_ref[...], w_ref[...], preferred_element_type=F32)

    @pl.when(j < n_rope_tiles)
    def _():
        for hh in range(acc.shape[1] // HEAD_DIM):
            t = acc[:, hh * HEAD_DIM:(hh + 1) * HEAD_DIM]
            up = pltpu.roll(t, HEAD_DIM - half, axis=1)
            dn = pltpu.roll(t, half, axis=1)
            r = t * cos_ref[...] + up * sup_ref[...] + dn * sdn_ref[...]
            o_ref[:, hh * HEAD_DIM:(hh + 1) * HEAD_DIM] = r.astype(o_ref.dtype)

    @pl.when(j >= n_rope_tiles)
    def _():
        o_ref[...] = acc.astype(o_ref.dtype)


def _qkv_proj(h, w_in_b, pos2, attn_width, kv_width, w_side):
    m, d = h.shape
    n = attn_width + 2 * kv_width
    assert attn_width % COL_TILE == 0 and kv_width % COL_TILE == 0 and COL_TILE % HEAD_DIM == 0
    n_rope_tiles = (attn_width + kv_width) // COL_TILE
    nj = n // COL_TILE
    side_spec, side_shape = _cast_spec(w_side, (m // ROW_TILE) * nj, lambda i, j: i * nj + j)
    return pl.pallas_call(
        functools.partial(_qkv_kernel, n_rope_tiles=n_rope_tiles),
        grid=(m // ROW_TILE, nj),
        in_specs=[pl.BlockSpec((ROW_TILE, d), lambda i, j: (i, 0)),
                  pl.BlockSpec((d, COL_TILE), lambda i, j: (0, j)),
                  pl.BlockSpec((ROW_TILE, 1), lambda i, j: (i, 0)),
                  side_spec],
        out_specs=[pl.BlockSpec((ROW_TILE, COL_TILE), lambda i, j: (i, j)), side_spec],
        out_shape=[jax.ShapeDtypeStruct((m, n), BF16), side_shape],
        scratch_shapes=[pltpu.VMEM((ROW_TILE, HEAD_DIM), F32)] * 3,
        compiler_params=_params("arbitrary", "arbitrary"),
        name="qkv_proj",
    )(h, w_in_b, pos2, w_side)


def _glu_kernel(h_ref, wa_ref, wb_ref, wsrc_ref, o_ref, wdst_ref):
    wdst_ref[...] = wsrc_ref[...].astype(wdst_ref.dtype)
    h = h_ref[...]
    a = jnp.dot(h, wa_ref[...], preferred_element_type=F32)
    b = jnp.dot(h, wb_ref[...], preferred_element_type=F32)
    o_ref[...] = (a * jax.nn.sigmoid(b)).astype(o_ref.dtype)


def _glu_proj(h, w, a_off, b_off, n_out, w_side):
    m, d = h.shape
    assert a_off % COL_TILE == 0 and b_off % COL_TILE == 0 and n_out % COL_TILE == 0
    ao, bo = a_off // COL_TILE, b_off // COL_TILE
    nj = n_out // COL_TILE
    side_spec, side_shape = _cast_spec(w_side, (m // ROW_TILE) * nj, lambda i, j: i * nj + j)
    return pl.pallas_call(
        _glu_kernel,
        grid=(m // ROW_TILE, nj),
        in_specs=[pl.BlockSpec((ROW_TILE, d), lambda i, j: (i, 0)),
                  pl.BlockSpec((d, COL_TILE), lambda i, j: (0, ao + j)),
                  pl.BlockSpec((d, COL_TILE), lambda i, j: (0, bo + j)),
                  side_spec],
        out_specs=[pl.BlockSpec((ROW_TILE, COL_TILE), lambda i, j: (i, j)), side_spec],
        out_shape=[jax.ShapeDtypeStruct((m, n_out), F32), side_shape],
        compiler_params=_params("arbitrary", "arbitrary"),
        name="conv_in_proj",
    )(h, w, w, w_side)


def _attn_kernel(sink_ref, q_ref, k0_ref, k1_ref, k2_ref, v0_ref, v1_ref, v2_ref, g_ref, wsrc_ref,
                 o_ref, wdst_ref, acc_ref, bias_ref, *, n_kv, q_per_kv):
    wdst_ref[...] = wsrc_ref[...].astype(wdst_ref.dtype)
    n = pl.program_id(1)
    nb = pl.num_programs(1)
    span = 3 * Q_TILE
    r = lax.broadcasted_iota(jnp.int32, (Q_TILE, span), 0)
    c = lax.broadcasted_iota(jnp.int32, (Q_TILE, span), 1)
    lo = jnp.where(n == 0, Q_TILE, 0)
    hi = jnp.where(n == nb - 1, 2 * Q_TILE, span)
    rel = c - Q_TILE - r
    valid = (rel >= -WINDOW) & (rel <= WINDOW) & (c >= lo) & (c < hi)
    bias_ref[...] = jnp.where(valid, 0.0, -jnp.inf)
    scale = HEAD_DIM ** -0.5
    for h in range(n_kv):
        hs = slice(h * HEAD_DIM, (h + 1) * HEAD_DIM)
        kh = jnp.concatenate([k0_ref[0, :, hs], k1_ref[0, :, hs], k2_ref[0, :, hs]], axis=0)
        vh = jnp.concatenate([v0_ref[0, :, hs], v1_ref[0, :, hs], v2_ref[0, :, hs]], axis=0)
        qh = jnp.concatenate(
            [q_ref[0, :, (h * q_per_kv + g) * HEAD_DIM:(h * q_per_kv + g + 1) * HEAD_DIM]
             for g in range(q_per_kv)], axis=0)
        s = lax.dot_general(qh, kh, (((1,), (1,)), ((), ())), preferred_element_type=F32)
        pn = []
        for g in range(q_per_kv):
            sink = sink_ref[h * q_per_kv + g]
            sg = s[g * Q_TILE:(g + 1) * Q_TILE, :] * scale + bias_ref[...]
            mx = jnp.maximum(jnp.max(sg, axis=-1, keepdims=True), sink)
            p = jnp.exp(sg - mx)
            denom = jnp.sum(p, axis=-1, keepdims=True) + jnp.exp(sink - mx)
            pn.append((p / denom).astype(vh.dtype))
        o = jnp.dot(jnp.concatenate(pn, axis=0), vh, preferred_element_type=F32)
        for g in range(q_per_kv):
            col = (h * q_per_kv + g) * HEAD_DIM
            acc_ref[:, col:col + HEAD_DIM] = o[g * Q_TILE:(g + 1) * Q_TILE, :]
    o_ref[0] = _rms(acc_ref[...], g_ref[...]).astype(o_ref.dtype)


def _attention(qkv3, sinks, g, attn_width, kv_width, w_side):
    b, s, _ = qkv3.shape
    nb = s // Q_TILE
    assert WINDOW == Q_TILE and s % Q_TILE == 0 and attn_width % kv_width == 0
    n_kv = kv_width // HEAD_DIM
    q_per_kv = attn_width // kv_width
    kcol = attn_width // kv_width
    vcol = kcol + 1

    def kv_spec(col, d):
        return pl.BlockSpec((1, Q_TILE, kv_width),
                            lambda bi, n: (bi, jnp.clip(n + d, 0, nb - 1), col))

    side_spec, side_shape = _cast_spec(w_side, b * nb, lambda bi, n: bi * nb + n)
    return pl.pallas_call(
        functools.partial(_attn_kernel, n_kv=n_kv, q_per_kv=q_per_kv),
        grid=(b, nb),
        in_specs=[pl.BlockSpec(memory_space=pltpu.SMEM),
                  pl.BlockSpec((1, Q_TILE, attn_width), lambda bi, n: (bi, n, 0)),
                  kv_spec(kcol, -1), kv_spec(kcol, 0), kv_spec(kcol, 1),
                  kv_spec(vcol, -1), kv_spec(vcol, 0), kv_spec(vcol, 1),
                  pl.BlockSpec((1, attn_width), lambda bi, n: (0, 0)),
                  side_spec],
        out_specs=[pl.BlockSpec((1, Q_TILE, attn_width), lambda bi, n: (bi, n, 0)), side_spec],
        out_shape=[jax.ShapeDtypeStruct((b, s, attn_width), BF16), side_shape],
        scratch_shapes=[pltpu.VMEM((Q_TILE, attn_width), F32), pltpu.VMEM((Q_TILE, 3 * Q_TILE), F32)],
        compiler_params=_params("arbitrary", "arbitrary"),
        name="swa_attention",
    )(sinks, qkv3, qkv3, qkv3, qkv3, qkv3, qkv3, qkv3, g, w_side)


CONV_RC = 64
CONV_CC = 128
SUBLANES = 8


def _conv_kernel(up_ref, u_ref, un_ref, wb_ref, cb_ref, lg_ref, lb_ref, og_ref, wsrc_ref,
                 o_ref, wdst_ref, slab_ref, y_ref, *, ksize):
    wdst_ref[...] = wsrc_ref[...].astype(wdst_ref.dtype)
    t = pl.program_id(1)
    nt = pl.num_programs(1)
    rows, ch = y_ref.shape
    halo = CONV_HALO
    slab_ref[0:halo, :] = jnp.where(t == 0, 0.0, up_ref[0])
    slab_ref[halo:halo + rows, :] = u_ref[0]
    slab_ref[halo + rows:halo + rows + halo, :] = jnp.where(t == nt - 1, 0.0, un_ref[0])
    off = halo - ksize // 2
    span = CONV_RC + 2 * halo
    n_rc = rows // CONV_RC

    def chunk(idx, carry):
        ci = idx // n_rc
        ri = idx - ci * n_rc
        cs = pl.ds(pl.multiple_of(ci * CONV_CC, CONV_CC), CONV_CC)
        r0 = pl.multiple_of(ri * CONV_RC, CONV_RC)
        raw = slab_ref[pl.ds(r0, span), cs]
        acc = jnp.zeros((CONV_RC, CONV_CC), F32)
        for sft in range(SUBLANES):
            sh = raw if sft == 0 else pltpu.roll(raw, span - sft, axis=0)
            for q in range(span // SUBLANES):
                k = q * SUBLANES + sft - off
                if 0 <= k < ksize:
                    wk = jnp.concatenate([wb_ref[k, :, cs]] * (CONV_RC // SUBLANES), axis=0)
                    acc = acc + sh[q * SUBLANES:q * SUBLANES + CONV_RC, :] * wk
        y_ref[pl.ds(r0, CONV_RC), cs] = acc + cb_ref[:, cs]
        return carry

    lax.fori_loop(0, (ch // CONV_CC) * n_rc, chunk, 0, unroll=2)

    y = y_ref[...]
    mu = jnp.mean(y, axis=-1, keepdims=True)
    d = y - mu
    var = jnp.mean(d * d, axis=-1, keepdims=True)
    z = d * lax.rsqrt(var + LN_EPS) * lg_ref[...] + lb_ref[...]
    sw = z * jax.nn.sigmoid(z)
    o_ref[0] = _rms(sw, og_ref[...]).astype(o_ref.dtype)


def _conv_branch(u3, conv_w, conv_b, ln_g, ln_b, out_g, w_side):
    b, s, ch = u3.shape
    ksize = conv_w.shape[0]
    assert ksize // 2 <= CONV_HALO and s % CONV_ROWS == 0 and ch % CONV_CC == 0
    w_bcast = jnp.broadcast_to(conv_w[:, None, :], (ksize, SUBLANES, ch))
    hb = CONV_ROWS // CONV_HALO
    last = s // CONV_HALO - 1
    vec = pl.BlockSpec((1, ch), lambda bi, t: (0, 0))
    nt = s // CONV_ROWS
    side_spec, side_shape = _cast_spec(w_side, b * nt, lambda bi, t: bi * nt + t)
    return pl.pallas_call(
        functools.partial(_conv_kernel, ksize=ksize),
        grid=(b, nt),
        in_specs=[pl.BlockSpec((1, CONV_HALO, ch), lambda bi, t: (bi, jnp.maximum(t * hb - 1, 0), 0)),
                  pl.BlockSpec((1, CONV_ROWS, ch), lambda bi, t: (bi, t, 0)),
                  pl.BlockSpec((1, CONV_HALO, ch), lambda bi, t: (bi, jnp.minimum((t + 1) * hb, last), 0)),
                  pl.BlockSpec((ksize, SUBLANES, ch), lambda bi, t: (0, 0, 0)),
                  vec, vec, vec, vec, side_spec],
        out_specs=[pl.BlockSpec((1, CONV_ROWS, ch), lambda bi, t: (bi, t, 0)), side_spec],
        out_shape=[jax.ShapeDtypeStruct((b, s, ch), BF16), side_shape],
        scratch_shapes=[pltpu.VMEM((CONV_ROWS + 2 * CONV_HALO, ch), F32),
                        pltpu.VMEM((CONV_ROWS, ch), F32)],
        compiler_params=_params("arbitrary", "arbitrary"),
        name="conv_branch",
    )(u3, u3, u3, w_bcast, conv_b, ln_g, ln_b, out_g, w_side)


def _outproj_kernel(a1_ref, a2_ref, w1_ref, w2_ref, o_ref):
    o_ref[...] = (jnp.dot(a1_ref[...], w1_ref[...], preferred_element_type=F32)
                  + jnp.dot(a2_ref[...], w2_ref[...], preferred_element_type=F32))


def _outproj(a1, a2, w_b):
    m, k1 = a1.shape
    k2 = a2.shape[1]
    n = w_b.shape[1]
    assert k1 == k2 and w_b.shape[0] == k1 + k2
    bn = 2 * COL_TILE
    return pl.pallas_call(
        _outproj_kernel,
        grid=(m // ROW_TILE, n // bn),
        in_specs=[pl.BlockSpec((ROW_TILE, k1), lambda i, j: (i, 0)),
                  pl.BlockSpec((ROW_TILE, k2), lambda i, j: (i, 0)),
                  pl.BlockSpec((k1, bn), lambda i, j: (0, j)),
                  pl.BlockSpec((k2, bn), lambda i, j: (1, j))],
        out_specs=pl.BlockSpec((ROW_TILE, bn), lambda i, j: (i, j)),
        out_shape=jax.ShapeDtypeStruct((m, n), F32),
        compiler_params=_params("parallel", "arbitrary"),
        name="outproj",
    )(a1, a2, w_b, w_b)


FF_TILE = 256
FIN_TILE = 512
DOWN_CHUNK = 512


def _ffn_kernel(h_ref, wg_ref, wu_ref, wd_ref, x1_ref, g_ref, o_ref, acc_ref, scale_ref, *, n_f):
    j = pl.program_id(1)
    d_model = acc_ref.shape[1]

    @pl.when(j == 0)
    def _():
        acc_ref[...] = jnp.zeros_like(acc_ref)

    @pl.when(j < n_f)
    def _():
        h = h_ref[...]
        gate = jnp.dot(h, wg_ref[...], preferred_element_type=F32)
        up = jnp.dot(h, wu_ref[...], preferred_element_type=F32)
        a = ((gate * jax.nn.sigmoid(gate)) * up).astype(h.dtype)
        for n0 in range(0, d_model, DOWN_CHUNK):
            acc_ref[:, n0:n0 + DOWN_CHUNK] += jnp.dot(a, wd_ref[:, n0:n0 + DOWN_CHUNK],
                                                       preferred_element_type=F32)

    @pl.when(j == n_f)
    def _():
        y = acc_ref[...]
        scale_ref[...] = lax.rsqrt(jnp.mean(y * y, axis=-1, keepdims=True) + RMS_EPS)

    @pl.when(j >= n_f)
    def _():
        cs = pl.ds(pl.multiple_of((j - n_f) * FIN_TILE, FIN_TILE), FIN_TILE)
        o_ref[...] = x1_ref[...] + acc_ref[:, cs] * scale_ref[...] * g_ref[...]


def _ffn(h, w_gate_b, w_up_b, w_down_b, x1, g):
    m, d = h.shape
    d_ff = w_gate_b.shape[1]
    assert d_ff % FF_TILE == 0 and d % FIN_TILE == 0 and d % DOWN_CHUNK == 0
    n_f = d_ff // FF_TILE
    n_fin = d // FIN_TILE
    ftile = lambda i, j: jnp.minimum(j, n_f - 1)
    fin = lambda i, j: jnp.maximum(j - n_f, 0)
    return pl.pallas_call(
        functools.partial(_ffn_kernel, n_f=n_f),
        grid=(m // ROW_TILE, n_f + n_fin),
        in_specs=[pl.BlockSpec((ROW_TILE, d), lambda i, j: (i, 0), pipeline_mode=pl.Buffered(1)),
                  pl.BlockSpec((d, FF_TILE), lambda i, j: (0, ftile(i, j))),
                  pl.BlockSpec((d, FF_TILE), lambda i, j: (0, ftile(i, j))),
                  pl.BlockSpec((FF_TILE, d), lambda i, j: (ftile(i, j), 0)),
                  pl.BlockSpec((ROW_TILE, FIN_TILE), lambda i, j: (i, fin(i, j))),
                  pl.BlockSpec((1, FIN_TILE), lambda i, j: (0, fin(i, j)))],
        out_specs=pl.BlockSpec((ROW_TILE, FIN_TILE), lambda i, j: (i, fin(i, j))),
        out_shape=jax.ShapeDtypeStruct((m, d), F32),
        scratch_shapes=[pltpu.VMEM((ROW_TILE, d), F32), pltpu.VMEM((ROW_TILE, 1), F32)],
        compiler_params=_params("parallel", "arbitrary"),
        name="ffn",
    )(h, w_gate_b, w_up_b, w_down_b, x1, g)


def kernel(x, positions, mix_pre_g, w_in, sinks, conv_w, conv_b, conv_ln_g, conv_ln_b, attn_out_g,
           conv_out_g, w_out, mix_post_g, ffn_pre_g, w_gate, w_up, w_down, ffn_post_g):
    b, s, d = x.shape
    depth = w_in.shape[0]
    attn_width = attn_out_g.shape[1]
    conv_width = conv_out_g.shape[1]
    kv_width = (w_in.shape[2] - attn_width - 2 * conv_width) // 2
    m = b * s
    assert m % ROW_TILE == 0 and m % NORM_ROWS == 0

    x2 = x.reshape(m, d)
    pos2 = positions.reshape(m, 1)
    row = lambda v: v.reshape(1, -1)

    for l in range(depth):
        w_in_b = w_in[l].astype(BF16)

        h1 = _prenorm(x2, row(mix_pre_g[l]))
        qkv, w_down_b = _qkv_proj(h1, w_in_b, pos2, attn_width, kv_width, w_down[l])
        qkv_n = attn_width + 2 * kv_width
        u, w_out_b = _glu_proj(h1, w_in_b, qkv_n, qkv_n + conv_width, conv_width, w_out[l])

        attn_n, w_gate_b = _attention(qkv.reshape(b, s, qkv_n), sinks[l], row(attn_out_g[l]), attn_width,
                                      kv_width, w_gate[l])
        conv_n, w_up_b = _conv_branch(u.reshape(b, s, conv_width), conv_w[l], row(conv_b[l]),
                                      row(conv_ln_g[l]), row(conv_ln_b[l]), row(conv_out_g[l]), w_up[l])

        y = _outproj(attn_n.reshape(m, attn_width), conv_n.reshape(m, conv_width), w_out_b)
        x1, h2 = _postmix(y, x2, row(mix_post_g[l]), row(ffn_pre_g[l]))

        x2 = _ffn(h2, w_gate_b, w_up_b, w_down_b, x1, row(ffn_post_g[l]))
    return x2.reshape(b, s, d)
```

```python
import functools
import math

import jax
import jax.numpy as jnp
from jax import lax
from jax.experimental import pallas as pl
from jax.experimental.pallas import tpu as pltpu

F32 = jnp.float32
BF16 = jnp.bfloat16

HEAD_DIM = 128
ROPE_DIM = HEAD_DIM // 4
ROPE_THETA = 500000.0
WINDOW = 128
RMS_EPS = 1e-6
LN_EPS = 1e-5

V7X_VMEM_BYTES = 64 * 1024 * 1024
VMEM_LIMIT = V7X_VMEM_BYTES - 6 * 1024 * 1024

ROW_TILE = 1024
COL_TILE = 512
NORM_ROWS = 256
Q_TILE = 128
CONV_ROWS = 256
CONV_HALO = 16


def _params(*sem):
    return pltpu.CompilerParams(dimension_semantics=sem, vmem_limit_bytes=VMEM_LIMIT)


def _rms(t, g):
    ms = jnp.mean(t * t, axis=-1, keepdims=True)
    return t * lax.rsqrt(ms + RMS_EPS) * g


BF16_ROWS = 16


def _cast_spec(w, n_steps, step):
    r, c = w.shape
    rows = next(d for d in range(BF16_ROWS, r + 1, BF16_ROWS) if r % d == 0 and d * n_steps >= r)
    last = r // rows - 1
    spec = pl.BlockSpec((rows, c), lambda *g: (jnp.minimum(step(*g), last), 0))
    return spec, jax.ShapeDtypeStruct((r, c), BF16)


def _prenorm_kernel(x_ref, g_ref, h_ref):
    h_ref[...] = _rms(x_ref[...], g_ref[...]).astype(h_ref.dtype)


def _prenorm(x2, g):
    m, d = x2.shape
    return pl.pallas_call(
        _prenorm_kernel,
        grid=(m // NORM_ROWS,),
        in_specs=[pl.BlockSpec((NORM_ROWS, d), lambda i: (i, 0)),
                  pl.BlockSpec((1, d), lambda i: (0, 0))],
        out_specs=pl.BlockSpec((NORM_ROWS, d), lambda i: (i, 0)),
        out_shape=jax.ShapeDtypeStruct((m, d), BF16),
        compiler_params=_params("parallel"),
        name="prenorm",
    )(x2, g)


def _postmix_kernel(y_ref, x_ref, g1_ref, g2_ref, x1_ref, h2_ref):
    x1 = x_ref[...] + _rms(y_ref[...], g1_ref[...])
    x1_ref[...] = x1
    h2_ref[...] = _rms(x1, g2_ref[...]).astype(h2_ref.dtype)


def _postmix(y, x2, g_post, g_pre):
    m, d = x2.shape
    row = pl.BlockSpec((NORM_ROWS, d), lambda i: (i, 0))
    vec = pl.BlockSpec((1, d), lambda i: (0, 0))
    return pl.pallas_call(
        _postmix_kernel,
        grid=(m // NORM_ROWS,),
        in_specs=[row, row, vec, vec],
        out_specs=[row, row],
        out_shape=[jax.ShapeDtypeStruct((m, d), F32), jax.ShapeDtypeStruct((m, d), BF16)],
        compiler_params=_params("parallel"),
        name="postmix",
    )(y, x2, g_post, g_pre)


ROPE_COLS = 256


def _qkv_kernel(h_ref, w_ref, pos_ref, wsrc_ref, o_ref, wdst_ref, cos_ref, sup_ref, sdn_ref, *, n_rope_tiles):
    j = pl.program_id(1)
    half = ROPE_DIM // 2
    wdst_ref[...] = wsrc_ref[...].astype(wdst_ref.dtype)

    @pl.when(j == 0)
    def _():
        lane = lax.broadcasted_iota(jnp.int32, (1, HEAD_DIM), 1)
        fidx = (lane % half).astype(F32)
        inv_freq = jnp.exp(fidx * (-2.0 * math.log(ROPE_THETA) / ROPE_DIM))
        ang = pos_ref[...].astype(F32) * inv_freq
        c = jnp.cos(ang)
        s = jnp.sin(ang)
        in_rope = lane < ROPE_DIM
        cos_ref[...] = jnp.where(in_rope, c, 1.0)
        sup_ref[...] = jnp.where(lane < half, -s, 0.0)
        sdn_ref[...] = jnp.where((lane >= half) & in_rope, s, 0.0)

    @pl.when(j < n_rope_tiles)
    def _():
        h = h_ref[...]
        for c0 in range(0, o_ref.shape[1], ROPE_COLS):
            acc = jnp.dot(h, w_ref[:, c0:c0 + ROPE_COLS], preferred_element_type=F32)
            for hh in range(ROPE_COLS // HEAD_DIM):
                t = acc[:, hh * HEAD_DIM:(hh + 1) * HEAD_DIM]
                up = pltpu.roll(t, HEAD_DIM - half, axis=1)
                dn = pltpu.roll(t, half, axis=1)
                r = t * cos_ref[...] + up * sup_ref[...] + dn * sdn_ref[...]
                o_ref[:, c0 + hh * HEAD_DIM:c0 + (hh + 1) * HEAD_DIM] = r.astype(o_ref.dtype)

    @pl.when(j >= n_rope_tiles)
    def _():
        o_ref[...] = jnp.dot(h_ref[...], w_ref[...], preferred_element_type=F32).astype(o_ref.dtype)


def _qkv_proj(h, w_in_b, pos2, attn_width, kv_width, w_side):
    m, d = h.shape
    n = attn_width + 2 * kv_width
    assert attn_width % COL_TILE == 0 and kv_width % COL_TILE == 0 and COL_TILE % HEAD_DIM == 0
    n_rope_tiles = (attn_width + kv_width) // COL_TILE
    nj = n // COL_TILE
    side_spec, side_shape = _cast_spec(w_side, (m // ROW_TILE) * nj, lambda i, j: i * nj + j)
    return pl.pallas_call(
        functools.partial(_qkv_kernel, n_rope_tiles=n_rope_tiles),
        grid=(m // ROW_TILE, nj),
        in_specs=[pl.BlockSpec((ROW_TILE, d), lambda i, j: (i, 0)),
                  pl.BlockSpec((d, COL_TILE), lambda i, j: (0, j)),
                  pl.BlockSpec((ROW_TILE, 1), lambda i, j: (i, 0)),
                  side_spec],
        out_specs=[pl.BlockSpec((ROW_TILE, COL_TILE), lambda i, j: (i, j)), side_spec],
        out_shape=[jax.ShapeDtypeStruct((m, n), BF16), side_shape],
        scratch_shapes=[pltpu.VMEM((ROW_TILE, HEAD_DIM), F32)] * 3,
        compiler_params=_params("arbitrary", "arbitrary"),
        name="qkv_proj",
    )(h, w_in_b, pos2, w_side)


GLU_COLS = 256


def _attn_glu_kernel(sink_ref, q_ref, k0_ref, k1_ref, k2_ref, v0_ref, v1_ref, v2_ref, g_ref,
                     h_ref, wa_ref, wb_ref, ws1_ref, ws2_ref,
                     o_ref, u_ref, wd1_ref, wd2_ref, acc_ref, bias_ref, *, n_kv, q_per_kv, nb):
    wd1_ref[...] = ws1_ref[...].astype(wd1_ref.dtype)
    wd2_ref[...] = ws2_ref[...].astype(wd2_ref.dtype)

    glu_rows = h_ref.shape[0] // n_kv

    n = lax.rem(pl.program_id(0) * pl.num_programs(1) + pl.program_id(1), nb)
    span = 3 * Q_TILE
    r = lax.broadcasted_iota(jnp.int32, (Q_TILE, span), 0)
    c = lax.broadcasted_iota(jnp.int32, (Q_TILE, span), 1)
    lo = jnp.where(n == 0, Q_TILE, 0)
    hi = jnp.where(n == nb - 1, 2 * Q_TILE, span)
    rel = c - Q_TILE - r
    valid = (rel >= -WINDOW) & (rel <= WINDOW) & (c >= lo) & (c < hi)
    bias_ref[...] = jnp.where(valid, 0.0, -jnp.inf)
    log2e = math.log2(math.e)
    scale2 = HEAD_DIM ** -0.5 * log2e
    def scores(h):
        hs = slice(h * HEAD_DIM, (h + 1) * HEAD_DIM)
        kh = jnp.concatenate([k0_ref[0, :, hs], k1_ref[0, :, hs], k2_ref[0, :, hs]], axis=0)
        qh = jnp.concatenate(
            [q_ref[0, :, (h * q_per_kv + g) * HEAD_DIM:(h * q_per_kv + g + 1) * HEAD_DIM]
             for g in range(q_per_kv)], axis=0)
        return lax.dot_general(qh, kh, (((1,), (1,)), ((), ())), preferred_element_type=F32)

    s_next = scores(0)
    for h in range(n_kv):
        s = s_next
        if h + 1 < n_kv:
            s_next = scores(h + 1)
        rs = slice(h * glu_rows, (h + 1) * glu_rows)
        h_in = h_ref[rs, :]
        ga = jnp.dot(h_in, wa_ref[...], preferred_element_type=F32)
        gb = jnp.dot(h_in, wb_ref[...], preferred_element_type=F32)
        u_ref[rs, :] = (ga * jax.nn.sigmoid(gb)).astype(u_ref.dtype)
        hs = slice(h * HEAD_DIM, (h + 1) * HEAD_DIM)
        vh = jnp.concatenate([v0_ref[0, :, hs], v1_ref[0, :, hs], v2_ref[0, :, hs]], axis=0)
        for g in range(q_per_kv):
            sink2 = sink_ref[h * q_per_kv + g] * log2e
            sg = s[g * Q_TILE:(g + 1) * Q_TILE, :] * scale2 + bias_ref[...]
            mx = jnp.maximum(jnp.max(sg, axis=-1, keepdims=True), sink2)
            p = jnp.exp2(sg - mx)
            denom = jnp.sum(p, axis=-1, keepdims=True) + jnp.exp2(sink2 - mx)
            pn = (p / denom).astype(vh.dtype)
            col = (h * q_per_kv + g) * HEAD_DIM
            acc_ref[:, col:col + HEAD_DIM] = jnp.dot(pn, vh, preferred_element_type=F32)
    o_ref[0] = _rms(acc_ref[...], g_ref[...]).astype(o_ref.dtype)


def _attn_glu(qkv3, sinks, g, attn_width, kv_width, h, w, a_off, b_off, n_out, w_side1, w_side2):
    b, s, _ = qkv3.shape
    m, d = h.shape
    nb = s // Q_TILE
    assert WINDOW == Q_TILE and s % Q_TILE == 0 and attn_width % kv_width == 0
    assert a_off % GLU_COLS == 0 and b_off % GLU_COLS == 0 and n_out % GLU_COLS == 0
    ni, nj = m // ROW_TILE, n_out // GLU_COLS
    assert ni * nj == b * nb
    n_kv = kv_width // HEAD_DIM
    q_per_kv = attn_width // kv_width
    kcol = attn_width // kv_width
    vcol = kcol + 1
    ao, bo = a_off // GLU_COLS, b_off // GLU_COLS
    step = lambda i, j: i * nj + j

    def blk(d_blk):
        return lambda i, j: (step(i, j) // nb, jnp.clip(step(i, j) % nb + d_blk, 0, nb - 1))

    def kv_spec(col, d_blk):
        return pl.BlockSpec((1, Q_TILE, kv_width), lambda i, j: (*blk(d_blk)(i, j), col))

    q_spec = pl.BlockSpec((1, Q_TILE, attn_width), lambda i, j: (*blk(0)(i, j), 0))
    s1_spec, s1_shape = _cast_spec(w_side1, ni * nj, step)
    s2_spec, s2_shape = _cast_spec(w_side2, ni * nj, step)
    return pl.pallas_call(
        functools.partial(_attn_glu_kernel, n_kv=n_kv, q_per_kv=q_per_kv, nb=nb),
        grid=(ni, nj),
        in_specs=[pl.BlockSpec(memory_space=pltpu.SMEM),
                  q_spec,
                  kv_spec(kcol, -1), kv_spec(kcol, 0), kv_spec(kcol, 1),
                  kv_spec(vcol, -1), kv_spec(vcol, 0), kv_spec(vcol, 1),
                  pl.BlockSpec((1, attn_width), lambda i, j: (0, 0)),
                  pl.BlockSpec((ROW_TILE, d), lambda i, j: (i, 0)),
                  pl.BlockSpec((d, GLU_COLS), lambda i, j: (0, ao + j)),
                  pl.BlockSpec((d, GLU_COLS), lambda i, j: (0, bo + j)),
                  s1_spec, s2_spec],
        out_specs=[q_spec, pl.BlockSpec((ROW_TILE, GLU_COLS), lambda i, j: (i, j)), s1_spec, s2_spec],
        out_shape=[jax.ShapeDtypeStruct((b, s, attn_width), BF16), jax.ShapeDtypeStruct((m, n_out), F32),
                   s1_shape, s2_shape],
        scratch_shapes=[pltpu.VMEM((Q_TILE, attn_width), F32), pltpu.VMEM((Q_TILE, 3 * Q_TILE), F32)],
        compiler_params=_params("arbitrary", "arbitrary"),
        name="attn_glu",
    )(sinks, qkv3, qkv3, qkv3, qkv3, qkv3, qkv3, qkv3, g, h, w, w, w_side1, w_side2)


CONV_RC = 64
CONV_CC = 128
SUBLANES = 8


def _conv_kernel(up_ref, u_ref, un_ref, wb_ref, cb_ref, lg_ref, lb_ref, og_ref, wsrc_ref,
                 o_ref, wdst_ref, slab_ref, y_ref, *, ksize):
    wdst_ref[...] = wsrc_ref[...].astype(wdst_ref.dtype)
    t = pl.program_id(1)
    nt = pl.num_programs(1)
    rows, ch = y_ref.shape
    halo = CONV_HALO
    slab_ref[0:halo, :] = jnp.where(t == 0, 0.0, up_ref[0])
    slab_ref[halo:halo + rows, :] = u_ref[0]
    slab_ref[halo + rows:halo + rows + halo, :] = jnp.where(t == nt - 1, 0.0, un_ref[0])
    off = halo - ksize // 2
    span = CONV_RC + 2 * halo
    n_rc = rows // CONV_RC

    def chunk(idx, carry):
        ci = idx // n_rc
        ri = idx - ci * n_rc
        cs = pl.ds(pl.multiple_of(ci * CONV_CC, CONV_CC), CONV_CC)
        r0 = pl.multiple_of(ri * CONV_RC, CONV_RC)
        raw = slab_ref[pl.ds(r0, span), cs]
        acc = jnp.zeros((CONV_RC, CONV_CC), F32)
        for sft in range(SUBLANES):
            sh = raw if sft == 0 else pltpu.roll(raw, span - sft, axis=0)
            for q in range(span // SUBLANES):
                k = q * SUBLANES + sft - off
                if 0 <= k < ksize:
                    wk = jnp.concatenate([wb_ref[k, :, cs]] * (CONV_RC // SUBLANES), axis=0)
                    acc = acc + sh[q * SUBLANES:q * SUBLANES + CONV_RC, :] * wk
        y_ref[pl.ds(r0, CONV_RC), cs] = acc + cb_ref[:, cs]
        return carry

    lax.fori_loop(0, (ch // CONV_CC) * n_rc, chunk, 0, unroll=2)

    y = y_ref[...]
    mu = jnp.mean(y, axis=-1, keepdims=True)
    d = y - mu
    var = jnp.mean(d * d, axis=-1, keepdims=True)
    z = d * lax.rsqrt(var + LN_EPS) * lg_ref[...] + lb_ref[...]
    sw = z * jax.nn.sigmoid(z)
    o_ref[0] = _rms(sw, og_ref[...]).astype(o_ref.dtype)


def _conv_branch(u3, conv_w, conv_b, ln_g, ln_b, out_g, w_side):
    b, s, ch = u3.shape
    ksize = conv_w.shape[0]
    assert ksize // 2 <= CONV_HALO and s % CONV_ROWS == 0 and ch % CONV_CC == 0
    w_bcast = jnp.broadcast_to(conv_w[:, None, :], (ksize, SUBLANES, ch))
    hb = CONV_ROWS // CONV_HALO
    last = s // CONV_HALO - 1
    vec = pl.BlockSpec((1, ch), lambda bi, t: (0, 0))
    nt = s // CONV_ROWS
    side_spec, side_shape = _cast_spec(w_side, b * nt, lambda bi, t: bi * nt + t)
    return pl.pallas_call(
        functools.partial(_conv_kernel, ksize=ksize),
        grid=(b, nt),
        in_specs=[pl.BlockSpec((1, CONV_HALO, ch), lambda bi, t: (bi, jnp.maximum(t * hb - 1, 0), 0)),
                  pl.BlockSpec((1, CONV_ROWS, ch), lambda bi, t: (bi, t, 0)),
                  pl.BlockSpec((1, CONV_HALO, ch), lambda bi, t: (bi, jnp.minimum((t + 1) * hb, last), 0)),
                  pl.BlockSpec((ksize, SUBLANES, ch), lambda bi, t: (0, 0, 0)),
                  vec, vec, vec, vec, side_spec],
        out_specs=[pl.BlockSpec((1, CONV_ROWS, ch), lambda bi, t: (bi, t, 0)), side_spec],
        out_shape=[jax.ShapeDtypeStruct((b, s, ch), BF16), side_shape],
        scratch_shapes=[pltpu.VMEM((CONV_ROWS + 2 * CONV_HALO, ch), F32),
                        pltpu.VMEM((CONV_ROWS, ch), F32)],
        compiler_params=_params("arbitrary", "arbitrary"),
        name="conv_branch",
    )(u3, u3, u3, w_bcast, conv_b, ln_g, ln_b, out_g, w_side)


def _outproj_kernel(a1_ref, a2_ref, w1_ref, w2_ref, o_ref):
    o_ref[...] = (jnp.dot(a1_ref[...], w1_ref[...], preferred_element_type=F32)
                  + jnp.dot(a2_ref[...], w2_ref[...], preferred_element_type=F32))


def _outproj(a1, a2, w_b):
    m, k1 = a1.shape
    k2 = a2.shape[1]
    n = w_b.shape[1]
    assert k1 == k2 and w_b.shape[0] == k1 + k2
    bn = 2 * COL_TILE
    return pl.pallas_call(
        _outproj_kernel,
        grid=(m // ROW_TILE, n // bn),
        in_specs=[pl.BlockSpec((ROW_TILE, k1), lambda i, j: (i, 0)),
                  pl.BlockSpec((ROW_TILE, k2), lambda i, j: (i, 0)),
                  pl.BlockSpec((k1, bn), lambda i, j: (0, j)),
                  pl.BlockSpec((k2, bn), lambda i, j: (1, j))],
        out_specs=pl.BlockSpec((ROW_TILE, bn), lambda i, j: (i, j)),
        out_shape=jax.ShapeDtypeStruct((m, n), F32),
        compiler_params=_params("parallel", "arbitrary"),
        name="outproj",
    )(a1, a2, w_b, w_b)


FF_TILE = 256
FIN_TILE = 512
DOWN_CHUNK = 512


def _ffn_kernel(h_ref, wg_ref, wu_ref, wd_ref, x1_ref, g_ref, o_ref, acc_ref, scale_ref, *, n_f):
    j = pl.program_id(1)
    d_model = acc_ref.shape[1]

    @pl.when((pl.program_id(0) == 0) & (j == 0))
    def _():
        acc_ref[...] = jnp.zeros_like(acc_ref)

    @pl.when(j < n_f)
    def _():
        h = h_ref[...]
        gate = jnp.dot(h, wg_ref[...], preferred_element_type=F32)
        up = jnp.dot(h, wu_ref[...], preferred_element_type=F32)
        a = ((gate * jax.nn.sigmoid(gate)) * up).astype(h.dtype)
        for n0 in range(0, d_model, DOWN_CHUNK):
            acc_ref[:, n0:n0 + DOWN_CHUNK] += jnp.dot(a, wd_ref[:, n0:n0 + DOWN_CHUNK],
                                                       preferred_element_type=F32)

    @pl.when(j == n_f)
    def _():
        y = acc_ref[...]
        scale_ref[...] = lax.rsqrt(jnp.mean(y * y, axis=-1, keepdims=True) + RMS_EPS)

    @pl.when(j >= n_f)
    def _():
        cs = pl.ds(pl.multiple_of((j - n_f) * FIN_TILE, FIN_TILE), FIN_TILE)
        o_ref[...] = x1_ref[...] + acc_ref[:, cs] * scale_ref[...] * g_ref[...]
        acc_ref[:, cs] = jnp.zeros((acc_ref.shape[0], FIN_TILE), F32)


def _ffn(h, w_gate_b, w_up_b, w_down_b, x1, g):
    m, d = h.shape
    d_ff = w_gate_b.shape[1]
    assert d_ff % FF_TILE == 0 and d % FIN_TILE == 0 and d % DOWN_CHUNK == 0
    n_f = d_ff // FF_TILE
    n_fin = d // FIN_TILE
    ftile = lambda i, j: jnp.minimum(j, n_f - 1)
    fin = lambda i, j: jnp.maximum(j - n_f, 0)
    return pl.pallas_call(
        functools.partial(_ffn_kernel, n_f=n_f),
        grid=(m // ROW_TILE, n_f + n_fin),
        in_specs=[pl.BlockSpec((ROW_TILE, d), lambda i, j: (i, 0)),
                  pl.BlockSpec((d, FF_TILE), lambda i, j: (0, ftile(i, j))),
                  pl.BlockSpec((d, FF_TILE), lambda i, j: (0, ftile(i, j))),
                  pl.BlockSpec((FF_TILE, d), lambda i, j: (ftile(i, j), 0)),
                  pl.BlockSpec((ROW_TILE, FIN_TILE), lambda i, j: (i, fin(i, j))),
                  pl.BlockSpec((1, FIN_TILE), lambda i, j: (0, fin(i, j)))],
        out_specs=pl.BlockSpec((ROW_TILE, FIN_TILE), lambda i, j: (i, fin(i, j))),
        out_shape=jax.ShapeDtypeStruct((m, d), F32),
        scratch_shapes=[pltpu.VMEM((ROW_TILE, d), F32), pltpu.VMEM((ROW_TILE, 1), F32)],
        compiler_params=_params("arbitrary", "arbitrary"),
        name="ffn",
    )(h, w_gate_b, w_up_b, w_down_b, x1, g)


def kernel(x, positions, mix_pre_g, w_in, sinks, conv_w, conv_b, conv_ln_g, conv_ln_b, attn_out_g,
           conv_out_g, w_out, mix_post_g, ffn_pre_g, w_gate, w_up, w_down, ffn_post_g):
    b, s, d = x.shape
    depth = w_in.shape[0]
    attn_width = attn_out_g.shape[1]
    conv_width = conv_out_g.shape[1]
    kv_width = (w_in.shape[2] - attn_width - 2 * conv_width) // 2
    m = b * s
    assert m % ROW_TILE == 0 and m % NORM_ROWS == 0

    x2 = x.reshape(m, d)
    pos2 = positions.reshape(m, 1)
    row = lambda v: v.reshape(1, -1)

    for l in range(depth):
        w_in_b = w_in[l].astype(BF16)

        h1 = _prenorm(x2, row(mix_pre_g[l]))
        qkv, w_down_b = _qkv_proj(h1, w_in_b, pos2, attn_width, kv_width, w_down[l])
        qkv_n = attn_width + 2 * kv_width
        attn_n, u, w_gate_b, w_out_b = _attn_glu(
            qkv.reshape(b, s, qkv_n), sinks[l], row(attn_out_g[l]), attn_width, kv_width,
            h1, w_in_b, qkv_n, qkv_n + conv_width, conv_width, w_gate[l], w_out[l])
        conv_n, w_up_b = _conv_branch(u.reshape(b, s, conv_width), conv_w[l], row(conv_b[l]),
                                      row(conv_ln_g[l]), row(conv_ln_b[l]), row(conv_out_g[l]), w_up[l])

        y = _outproj(attn_n.reshape(m, attn_width), conv_n.reshape(m, conv_width), w_out_b)
        x1, h2 = _postmix(y, x2, row(mix_post_g[l]), row(ffn_pre_g[l]))

        x2 = _ffn(h2, w_gate_b, w_up_b, w_down_b, x1, row(ffn_post_g[l]))
    return x2.reshape(b, s, d)
```

```python
import functools
import math

import jax
import jax.numpy as jnp
from jax import lax
from jax.experimental import pallas as pl
from jax.experimental.pallas import tpu as pltpu

F32 = jnp.float32
BF16 = jnp.bfloat16

HEAD_DIM = 128
ROPE_DIM = HEAD_DIM // 4
ROPE_THETA = 500000.0
WINDOW = 128
RMS_EPS = 1e-6
LN_EPS = 1e-5

V7X_VMEM_BYTES = 64 * 1024 * 1024
VMEM_LIMIT = V7X_VMEM_BYTES - 6 * 1024 * 1024

ROW_TILE = 1024
COL_TILE = 512
NORM_ROWS = 256
Q_TILE = 128
CONV_HALO = 16


def _params(*sem, flags=None):
    return pltpu.CompilerParams(dimension_semantics=sem, vmem_limit_bytes=VMEM_LIMIT, flags=flags)


def _rms(t, g):
    ms = jnp.mean(t * t, axis=-1, keepdims=True)
    return t * lax.rsqrt(ms + RMS_EPS) * g


BF16_ROWS = 16


def _cast_spec(w, n_steps, step):
    r, c = w.shape
    rows = next(d for d in range(BF16_ROWS, r + 1, BF16_ROWS) if r % d == 0 and d * n_steps >= r)
    last = r // rows - 1
    spec = pl.BlockSpec((rows, c), lambda *g: (jnp.minimum(step(*g), last), 0))
    return spec, jax.ShapeDtypeStruct((r, c), BF16)


def _prenorm_kernel(x_ref, g_ref, h_ref):
    h_ref[...] = _rms(x_ref[...], g_ref[...]).astype(h_ref.dtype)


def _prenorm(x2, g):
    m, d = x2.shape
    return pl.pallas_call(
        _prenorm_kernel,
        grid=(m // NORM_ROWS,),
        in_specs=[pl.BlockSpec((NORM_ROWS, d), lambda i: (i, 0)),
                  pl.BlockSpec((1, d), lambda i: (0, 0))],
        out_specs=pl.BlockSpec((NORM_ROWS, d), lambda i: (i, 0)),
        out_shape=jax.ShapeDtypeStruct((m, d), BF16),
        compiler_params=_params("parallel"),
        name="prenorm",
    )(x2, g)


def _postmix_kernel(y_ref, x_ref, g1_ref, g2_ref, x1_ref, h2_ref):
    x1 = x_ref[...] + _rms(y_ref[...], g1_ref[...])
    x1_ref[...] = x1
    h2_ref[...] = _rms(x1, g2_ref[...]).astype(h2_ref.dtype)


def _postmix(y, x2, g_post, g_pre):
    m, d = x2.shape
    row = pl.BlockSpec((NORM_ROWS, d), lambda i: (i, 0))
    vec = pl.BlockSpec((1, d), lambda i: (0, 0))
    return pl.pallas_call(
        _postmix_kernel,
        grid=(m // NORM_ROWS,),
        in_specs=[row, row, vec, vec],
        out_specs=[row, row],
        out_shape=[jax.ShapeDtypeStruct((m, d), F32), jax.ShapeDtypeStruct((m, d), BF16)],
        compiler_params=_params("parallel"),
        name="postmix",
    )(y, x2, g_post, g_pre)


ROPE_COLS = 256


def _qkv_kernel(h_ref, w_ref, pos_ref, wsrc_ref, o_ref, wdst_ref, cos_ref, sup_ref, sdn_ref, *, n_rope_tiles):
    j = pl.program_id(1)
    half = ROPE_DIM // 2
    wdst_ref[...] = wsrc_ref[...].astype(wdst_ref.dtype)

    @pl.when(j == 0)
    def _():
        lane = lax.broadcasted_iota(jnp.int32, (1, HEAD_DIM), 1)
        fidx = (lane % half).astype(F32)
        inv_freq = jnp.exp(fidx * (-2.0 * math.log(ROPE_THETA) / ROPE_DIM))
        ang = pos_ref[...].astype(F32) * inv_freq
        c = jnp.cos(ang)
        s = jnp.sin(ang)
        in_rope = lane < ROPE_DIM
        cos_ref[...] = jnp.where(in_rope, c, 1.0)
        sup_ref[...] = jnp.where(lane < half, -s, 0.0)
        sdn_ref[...] = jnp.where((lane >= half) & in_rope, s, 0.0)

    @pl.when(j < n_rope_tiles)
    def _():
        h = h_ref[...]
        for c0 in range(0, o_ref.shape[1], ROPE_COLS):
            acc = jnp.dot(h, w_ref[:, c0:c0 + ROPE_COLS], preferred_element_type=F32)
            for hh in range(ROPE_COLS // HEAD_DIM):
                t = acc[:, hh * HEAD_DIM:(hh + 1) * HEAD_DIM]
                up = pltpu.roll(t, HEAD_DIM - half, axis=1)
                dn = pltpu.roll(t, half, axis=1)
                r = t * cos_ref[...] + up * sup_ref[...] + dn * sdn_ref[...]
                o_ref[:, c0 + hh * HEAD_DIM:c0 + (hh + 1) * HEAD_DIM] = r.astype(o_ref.dtype)

    @pl.when(j >= n_rope_tiles)
    def _():
        o_ref[...] = jnp.dot(h_ref[...], w_ref[...], preferred_element_type=F32).astype(o_ref.dtype)


def _qkv_proj(h, w_in_b, pos2, attn_width, kv_width, w_side):
    m, d = h.shape
    n = attn_width + 2 * kv_width
    assert attn_width % COL_TILE == 0 and kv_width % COL_TILE == 0 and COL_TILE % HEAD_DIM == 0
    n_rope_tiles = (attn_width + kv_width) // COL_TILE
    nj = n // COL_TILE
    side_spec, side_shape = _cast_spec(w_side, (m // ROW_TILE) * nj, lambda i, j: i * nj + j)
    return pl.pallas_call(
        functools.partial(_qkv_kernel, n_rope_tiles=n_rope_tiles),
        grid=(m // ROW_TILE, nj),
        in_specs=[pl.BlockSpec((ROW_TILE, d), lambda i, j: (i, 0)),
                  pl.BlockSpec((d, COL_TILE), lambda i, j: (0, j)),
                  pl.BlockSpec((ROW_TILE, 1), lambda i, j: (i, 0)),
                  side_spec],
        out_specs=[pl.BlockSpec((ROW_TILE, COL_TILE), lambda i, j: (i, j)), side_spec],
        out_shape=[jax.ShapeDtypeStruct((m, n), BF16), side_shape],
        scratch_shapes=[pltpu.VMEM((ROW_TILE, HEAD_DIM), F32)] * 3,
        compiler_params=_params("arbitrary", "arbitrary"),
        name="qkv_proj",
    )(h, w_in_b, pos2, w_side)


GLU_COLS = 256


def _attn_glu_kernel(sink_ref, q_ref, k0_ref, k1_ref, k2_ref, v0_ref, v1_ref, v2_ref, g_ref,
                     h_ref, wa_ref, wb_ref, ws1_ref, ws2_ref,
                     o_ref, u_ref, wd1_ref, wd2_ref, acc_ref, bias_ref, *, n_kv, q_per_kv, nb):
    wd1_ref[...] = ws1_ref[...].astype(wd1_ref.dtype)
    wd2_ref[...] = ws2_ref[...].astype(wd2_ref.dtype)

    glu_rows = h_ref.shape[0] // n_kv

    n = lax.rem(pl.program_id(0) * pl.num_programs(1) + pl.program_id(1), nb)
    span = 3 * Q_TILE
    r = lax.broadcasted_iota(jnp.int32, (Q_TILE, span), 0)
    c = lax.broadcasted_iota(jnp.int32, (Q_TILE, span), 1)
    lo = jnp.where(n == 0, Q_TILE, 0)
    hi = jnp.where(n == nb - 1, 2 * Q_TILE, span)
    rel = c - Q_TILE - r
    valid = (rel >= -WINDOW) & (rel <= WINDOW) & (c >= lo) & (c < hi)
    bias_ref[...] = jnp.where(valid, 0.0, -jnp.inf)
    log2e = math.log2(math.e)
    scale2 = HEAD_DIM ** -0.5 * log2e
    def scores(h):
        hs = slice(h * HEAD_DIM, (h + 1) * HEAD_DIM)
        kh = jnp.concatenate([k0_ref[0, :, hs], k1_ref[0, :, hs], k2_ref[0, :, hs]], axis=0)
        qh = jnp.concatenate(
            [q_ref[0, :, (h * q_per_kv + g) * HEAD_DIM:(h * q_per_kv + g + 1) * HEAD_DIM]
             for g in range(q_per_kv)], axis=0)
        return lax.dot_general(qh, kh, (((1,), (1,)), ((), ())), preferred_element_type=F32)

    s_next = scores(0)
    for h in range(n_kv):
        s = s_next
        if h + 1 < n_kv:
            s_next = scores(h + 1)
        rs = slice(h * glu_rows, (h + 1) * glu_rows)
        h_in = h_ref[rs, :]
        ga = jnp.dot(h_in, wa_ref[...], preferred_element_type=F32)
        gb = jnp.dot(h_in, wb_ref[...], preferred_element_type=F32)
        u_ref[rs, :] = (ga * jax.nn.sigmoid(gb)).astype(u_ref.dtype)
        hs = slice(h * HEAD_DIM, (h + 1) * HEAD_DIM)
        vh = jnp.concatenate([v0_ref[0, :, hs], v1_ref[0, :, hs], v2_ref[0, :, hs]], axis=0)
        for g in range(q_per_kv):
            sink2 = sink_ref[h * q_per_kv + g] * log2e
            sg = s[g * Q_TILE:(g + 1) * Q_TILE, :] * scale2 + bias_ref[...]
            mx = jnp.maximum(jnp.max(sg, axis=-1, keepdims=True), sink2)
            p = jnp.exp2(sg - mx)
            denom = jnp.sum(p, axis=-1, keepdims=True) + jnp.exp2(sink2 - mx)
            pn = (p / denom).astype(vh.dtype)
            col = (h * q_per_kv + g) * HEAD_DIM
            acc_ref[:, col:col + HEAD_DIM] = jnp.dot(pn, vh, preferred_element_type=F32)
    o_ref[0] = _rms(acc_ref[...], g_ref[...]).astype(o_ref.dtype)


def _attn_glu(qkv3, sinks, g, attn_width, kv_width, h, w, a_off, b_off, n_out, w_side1, w_side2):
    b, s, _ = qkv3.shape
    m, d = h.shape
    nb = s // Q_TILE
    assert WINDOW == Q_TILE and s % Q_TILE == 0 and attn_width % kv_width == 0
    assert a_off % GLU_COLS == 0 and b_off % GLU_COLS == 0 and n_out % GLU_COLS == 0
    ni, nj = m // ROW_TILE, n_out // GLU_COLS
    assert ni * nj == b * nb
    n_kv = kv_width // HEAD_DIM
    q_per_kv = attn_width // kv_width
    kcol = attn_width // kv_width
    vcol = kcol + 1
    ao, bo = a_off // GLU_COLS, b_off // GLU_COLS
    step = lambda i, j: i * nj + j

    def blk(d_blk):
        return lambda i, j: (step(i, j) // nb, jnp.clip(step(i, j) % nb + d_blk, 0, nb - 1))

    def kv_spec(col, d_blk):
        return pl.BlockSpec((1, Q_TILE, kv_width), lambda i, j: (*blk(d_blk)(i, j), col))

    q_spec = pl.BlockSpec((1, Q_TILE, attn_width), lambda i, j: (*blk(0)(i, j), 0))
    s1_spec, s1_shape = _cast_spec(w_side1, ni * nj, step)
    s2_spec, s2_shape = _cast_spec(w_side2, ni * nj, step)
    return pl.pallas_call(
        functools.partial(_attn_glu_kernel, n_kv=n_kv, q_per_kv=q_per_kv, nb=nb),
        grid=(ni, nj),
        in_specs=[pl.BlockSpec(memory_space=pltpu.SMEM),
                  q_spec,
                  kv_spec(kcol, -1), kv_spec(kcol, 0), kv_spec(kcol, 1),
                  kv_spec(vcol, -1), kv_spec(vcol, 0), kv_spec(vcol, 1),
                  pl.BlockSpec((1, attn_width), lambda i, j: (0, 0)),
                  pl.BlockSpec((ROW_TILE, d), lambda i, j: (i, 0)),
                  pl.BlockSpec((d, GLU_COLS), lambda i, j: (0, ao + j)),
                  pl.BlockSpec((d, GLU_COLS), lambda i, j: (0, bo + j)),
                  s1_spec, s2_spec],
        out_specs=[q_spec, pl.BlockSpec((ROW_TILE, GLU_COLS), lambda i, j: (i, j)), s1_spec, s2_spec],
        out_shape=[jax.ShapeDtypeStruct((b, s, attn_width), BF16), jax.ShapeDtypeStruct((m, n_out), F32),
                   s1_shape, s2_shape],
        scratch_shapes=[pltpu.VMEM((Q_TILE, attn_width), F32), pltpu.VMEM((Q_TILE, 3 * Q_TILE), F32)],
        compiler_params=_params("arbitrary", "arbitrary"),
        name="attn_glu",
    )(sinks, qkv3, qkv3, qkv3, qkv3, qkv3, qkv3, qkv3, g, h, w, w, w_side1, w_side2)


CONV_RC = 16
CONV_SUMS = 2
CONV_CC = 128
SUBLANES = 8


def _conv_fill_slab(t, nt, up_ref, u_ref, un_ref, slab_ref):
    rows = u_ref.shape[1]
    halo = CONV_HALO
    slab_ref[0:halo, :] = jnp.where(t == 0, 0.0, up_ref[0])
    slab_ref[halo:halo + rows, :] = u_ref[0]
    slab_ref[halo + rows:halo + rows + halo, :] = jnp.where(t == nt - 1, 0.0, un_ref[0])


def _exact_zero_of(v):
    bits = pltpu.bitcast(v, jnp.uint32)
    return pltpu.bitcast((bits >> 16) >> 16, F32)


def _conv_chunk(r0, c0, wb_ref, cb_ref, slab_ref, y_ref, ksize, after=None):
    cs = slice(c0, c0 + CONV_CC)
    off = CONV_HALO - ksize // 2
    span = CONV_RC + 2 * CONV_HALO
    raw = slab_ref[r0:r0 + span, cs]
    if after is None:
        zero = jnp.zeros((CONV_RC, CONV_CC), F32)
    else:
        zero = jnp.concatenate([_exact_zero_of(after)] * (CONV_RC // SUBLANES), axis=0)
    accs = [zero] * CONV_SUMS
    n = 0
    for sft in range(SUBLANES):
        sh = raw if sft == 0 else pltpu.roll(raw, span - sft, axis=0)
        for q in range(span // SUBLANES):
            k = q * SUBLANES + sft - off
            if 0 <= k < ksize:
                wk = jnp.concatenate([wb_ref[k, :, cs]] * (CONV_RC // SUBLANES), axis=0)
                accs[n % CONV_SUMS] = accs[n % CONV_SUMS] + sh[q * SUBLANES:q * SUBLANES + CONV_RC, :] * wk
                n += 1
    out = functools.reduce(lambda a, b: a + b, accs) + cb_ref[:, cs]
    y_ref[r0:r0 + CONV_RC, cs] = out
    return out[0:SUBLANES, :]


def _conv_norms(y_ref, lg_ref, lb_ref, og_ref):
    y = y_ref[...]
    mu = jnp.mean(y, axis=-1, keepdims=True)
    d = y - mu
    var = jnp.mean(d * d, axis=-1, keepdims=True)
    z = d * lax.rsqrt(var + LN_EPS) * lg_ref[...] + lb_ref[...]
    sw = z * jax.nn.sigmoid(z)
    return _rms(sw, og_ref[...])


OUT_COLS = 512
MXU_PIECES = 4


def _conv_out_kernel(a1_ref, w1_ref, w2_ref, up_ref, u_ref, un_ref, wb_ref, cb_ref, lg_ref, lb_ref, og_ref,
                     wsrc_ref, y_out_ref, wdst_ref, convn_ref, slab_ref, ycv_ref, *, ksize, nt):
    i = pl.program_id(0)
    j = pl.program_id(1)
    ni = pl.num_programs(0) - 1
    nj = pl.num_programs(1)
    wdst_ref[...] = wsrc_ref[...].astype(wdst_ref.dtype)
    slot = lax.rem(i, 2)
    rows, ch = ycv_ref.shape
    chunks = [(r0, c0) for c0 in range(0, ch, CONV_CC) for r0 in range(0, rows, CONV_RC)]
    pc = y_out_ref.shape[1] // (MXU_PIECES // 2)
    per = -(-len(chunks) // MXU_PIECES)

    def step(do_mm, do_conv):
        if do_conv:
            t = lax.rem(i * nj + j, nt)
            _conv_fill_slab(t, nt, up_ref, u_ref, un_ref, slab_ref)
        if do_mm:
            a1 = a1_ref[...]
            a2 = convn_ref[1 - slot]
        dep = None
        for p in range(MXU_PIECES):
            if do_mm:
                c = (p // 2) * pc
                lhs, w_ref = (a1, w1_ref) if p % 2 == 0 else (a2, w2_ref)
                part = jnp.dot(lhs, w_ref[:, c:c + pc], preferred_element_type=F32)
                if p % 2 == 0:
                    y_out_ref[:, c:c + pc] = part
                else:
                    y_out_ref[:, c:c + pc] += part
            if do_conv:
                for r0, c0 in chunks[p * per:(p + 1) * per]:
                    out = _conv_chunk(r0, c0, wb_ref, cb_ref, slab_ref, ycv_ref, ksize, after=dep)
                    dep = out if do_mm else None
        if do_conv:
            cn = _conv_norms(ycv_ref, lg_ref, lb_ref, og_ref)
            convn_ref[slot, pl.ds(pl.multiple_of(j * rows, rows), rows), :] = cn.astype(convn_ref.dtype)

    @pl.when(i == 0)
    def _():
        y_out_ref[...] = jnp.zeros_like(y_out_ref)
        step(False, True)

    @pl.when((i > 0) & (i < ni))
    def _():
        step(True, True)

    @pl.when(i == ni)
    def _():
        step(True, False)


def _conv_outproj(a1, u3, conv_w, conv_b, ln_g, ln_b, out_g, w_b, w_side):
    m, k1 = a1.shape
    b, s, ch = u3.shape
    n = w_b.shape[1]
    ksize = conv_w.shape[0]
    ni, nj = m // ROW_TILE, n // OUT_COLS
    assert w_b.shape[0] == k1 + ch and k1 == ch and ROW_TILE % nj == 0
    rows = ROW_TILE // nj
    assert ksize // 2 <= CONV_HALO and s % rows == 0 and rows % CONV_RC == 0 and ch % CONV_CC == 0
    nt = s // rows
    w_bcast = jnp.broadcast_to(conv_w[:, None, :], (ksize, SUBLANES, ch))
    hb = rows // CONV_HALO
    last = s // CONV_HALO - 1
    prev = lambda i: jnp.maximum(i - 1, 0)

    def cblk(i, j):
        g = jnp.minimum(i, ni - 1) * nj + j
        return g // nt, g % nt

    vec = pl.BlockSpec((1, ch), lambda i, j: (0, 0))
    side_spec, side_shape = _cast_spec(w_side, (ni + 1) * nj, lambda i, j: i * nj + j)
    return pl.pallas_call(
        functools.partial(_conv_out_kernel, ksize=ksize, nt=nt),
        grid=(ni + 1, nj),
        in_specs=[pl.BlockSpec((ROW_TILE, k1), lambda i, j: (prev(i), 0)),
                  pl.BlockSpec((k1, OUT_COLS), lambda i, j: (0, j)),
                  pl.BlockSpec((ch, OUT_COLS), lambda i, j: (1, j)),
                  pl.BlockSpec((1, CONV_HALO, ch),
                               lambda i, j: (cblk(i, j)[0], jnp.maximum(cblk(i, j)[1] * hb - 1, 0), 0)),
                  pl.BlockSpec((1, rows, ch), lambda i, j: (*cblk(i, j), 0)),
                  pl.BlockSpec((1, CONV_HALO, ch),
                               lambda i, j: (cblk(i, j)[0], jnp.minimum((cblk(i, j)[1] + 1) * hb, last), 0)),
                  pl.BlockSpec((ksize, SUBLANES, ch), lambda i, j: (0, 0, 0)),
                  vec, vec, vec, vec, side_spec],
        out_specs=[pl.BlockSpec((ROW_TILE, OUT_COLS), lambda i, j: (prev(i), jnp.where(i > 0, j, 0))),
                   side_spec],
        out_shape=[jax.ShapeDtypeStruct((m, n), F32), side_shape],
        scratch_shapes=[pltpu.VMEM((2, ROW_TILE, ch), BF16),
                        pltpu.VMEM((rows + 2 * CONV_HALO, ch), F32),
                        pltpu.VMEM((rows, ch), F32)],
        compiler_params=_params("arbitrary", "arbitrary"),
        name="conv_outproj",
    )(a1, w_b, w_b, u3, u3, u3, w_bcast, conv_b, ln_g, ln_b, out_g, w_side)


FF_TILE = 256
FIN_TILE = 512
DOWN_CHUNK = 512


def _ffn_kernel(h_ref, wg_ref, wu_ref, wd_ref, x1_ref, g_ref, o_ref, acc_ref, scale_ref, *, n_f):
    j = pl.program_id(1)
    d_model = acc_ref.shape[1]

    @pl.when((pl.program_id(0) == 0) & (j == 0))
    def _():
        acc_ref[...] = jnp.zeros_like(acc_ref)

    @pl.when(j < n_f)
    def _():
        h = h_ref[...]
        gate = jnp.dot(h, wg_ref[...], preferred_element_type=F32)
        up = jnp.dot(h, wu_ref[...], preferred_element_type=F32)
        a = ((gate * jax.nn.sigmoid(gate)) * up).astype(h.dtype)
        for n0 in range(0, d_model, DOWN_CHUNK):
            acc_ref[:, n0:n0 + DOWN_CHUNK] += jnp.dot(a, wd_ref[:, n0:n0 + DOWN_CHUNK],
                                                       preferred_element_type=F32)

    @pl.when(j == n_f)
    def _():
        y = acc_ref[...]
        scale_ref[...] = lax.rsqrt(jnp.mean(y * y, axis=-1, keepdims=True) + RMS_EPS)

    @pl.when(j >= n_f)
    def _():
        cs = pl.ds(pl.multiple_of((j - n_f) * FIN_TILE, FIN_TILE), FIN_TILE)
        o_ref[...] = x1_ref[...] + acc_ref[:, cs] * scale_ref[...] * g_ref[...]
        acc_ref[:, cs] = jnp.zeros((acc_ref.shape[0], FIN_TILE), F32)


def _ffn(h, w_gate_b, w_up_b, w_down_b, x1, g):
    m, d = h.shape
    d_ff = w_gate_b.shape[1]
    assert d_ff % FF_TILE == 0 and d % FIN_TILE == 0 and d % DOWN_CHUNK == 0
    n_f = d_ff // FF_TILE
    n_fin = d // FIN_TILE
    ftile = lambda i, j: jnp.minimum(j, n_f - 1)
    fin = lambda i, j: jnp.maximum(j - n_f, 0)
    return pl.pallas_call(
        functools.partial(_ffn_kernel, n_f=n_f),
        grid=(m // ROW_TILE, n_f + n_fin),
        in_specs=[pl.BlockSpec((ROW_TILE, d), lambda i, j: (i, 0)),
                  pl.BlockSpec((d, FF_TILE), lambda i, j: (0, ftile(i, j))),
                  pl.BlockSpec((d, FF_TILE), lambda i, j: (0, ftile(i, j))),
                  pl.BlockSpec((FF_TILE, d), lambda i, j: (ftile(i, j), 0)),
                  pl.BlockSpec((ROW_TILE, FIN_TILE), lambda i, j: (i, fin(i, j))),
                  pl.BlockSpec((1, FIN_TILE), lambda i, j: (0, fin(i, j)))],
        out_specs=pl.BlockSpec((ROW_TILE, FIN_TILE), lambda i, j: (i, fin(i, j))),
        out_shape=jax.ShapeDtypeStruct((m, d), F32),
        scratch_shapes=[pltpu.VMEM((ROW_TILE, d), F32), pltpu.VMEM((ROW_TILE, 1), F32)],
        compiler_params=_params("arbitrary", "arbitrary"),
        name="ffn",
    )(h, w_gate_b, w_up_b, w_down_b, x1, g)


def kernel(x, positions, mix_pre_g, w_in, sinks, conv_w, conv_b, conv_ln_g, conv_ln_b, attn_out_g,
           conv_out_g, w_out, mix_post_g, ffn_pre_g, w_gate, w_up, w_down, ffn_post_g):
    b, s, d = x.shape
    depth = w_in.shape[0]
    attn_width = attn_out_g.shape[1]
    conv_width = conv_out_g.shape[1]
    kv_width = (w_in.shape[2] - attn_width - 2 * conv_width) // 2
    m = b * s
    assert m % ROW_TILE == 0 and m % NORM_ROWS == 0

    x2 = x.reshape(m, d)
    pos2 = positions.reshape(m, 1)
    row = lambda v: v.reshape(1, -1)

    for l in range(depth):
        w_in_b = w_in[l].astype(BF16)

        h1 = _prenorm(x2, row(mix_pre_g[l]))
        qkv, w_down_b = _qkv_proj(h1, w_in_b, pos2, attn_width, kv_width, w_down[l])
        qkv_n = attn_width + 2 * kv_width
        attn_n, u, w_gate_b, w_out_b = _attn_glu(
            qkv.reshape(b, s, qkv_n), sinks[l], row(attn_out_g[l]), attn_width, kv_width,
            h1, w_in_b, qkv_n, qkv_n + conv_width, conv_width, w_gate[l], w_out[l])
        y, w_up_b = _conv_outproj(attn_n.reshape(m, attn_width), u.reshape(b, s, conv_width), conv_w[l],
                                  row(conv_b[l]), row(conv_ln_g[l]), row(conv_ln_b[l]), row(conv_out_g[l]),
                                  w_out_b, w_up[l])
        x1, h2 = _postmix(y, x2, row(mix_post_g[l]), row(ffn_pre_g[l]))

        x2 = _ffn(h2, w_gate_b, w_up_b, w_down_b, x1, row(ffn_post_g[l]))
    return x2.reshape(b, s, d)
```

```python
import functools
import math

import jax
import jax.numpy as jnp
from jax import lax
from jax.experimental import pallas as pl
from jax.experimental.pallas import tpu as pltpu

F32 = jnp.float32
BF16 = jnp.bfloat16

HEAD_DIM = 128
ROPE_DIM = HEAD_DIM // 4
ROPE_THETA = 500000.0
WINDOW = 128
RMS_EPS = 1e-6
LN_EPS = 1e-5

V7X_VMEM_BYTES = 64 * 1024 * 1024
VMEM_LIMIT = V7X_VMEM_BYTES - 6 * 1024 * 1024

ROW_TILE = 1024
COL_TILE = 512
NORM_ROWS = 256
Q_TILE = 128
CONV_HALO = 16


def _params(*sem, flags=None):
    return pltpu.CompilerParams(dimension_semantics=sem, vmem_limit_bytes=VMEM_LIMIT, flags=flags)


def _rms(t, g):
    ms = jnp.mean(t * t, axis=-1, keepdims=True)
    return t * lax.rsqrt(ms + RMS_EPS) * g


BF16_ROWS = 16


def _cast_spec(w, n_steps, step):
    r, c = w.shape
    rows = next(d for d in range(BF16_ROWS, r + 1, BF16_ROWS) if r % d == 0 and d * n_steps >= r)
    last = r // rows - 1
    spec = pl.BlockSpec((rows, c), lambda *g: (jnp.minimum(step(*g), last), 0))
    return spec, jax.ShapeDtypeStruct((r, c), BF16)


def _prenorm_kernel(x_ref, g_ref, h_ref):
    h_ref[...] = _rms(x_ref[...], g_ref[...]).astype(h_ref.dtype)


def _prenorm(x2, g):
    m, d = x2.shape
    return pl.pallas_call(
        _prenorm_kernel,
        grid=(m // NORM_ROWS,),
        in_specs=[pl.BlockSpec((NORM_ROWS, d), lambda i: (i, 0)),
                  pl.BlockSpec((1, d), lambda i: (0, 0))],
        out_specs=pl.BlockSpec((NORM_ROWS, d), lambda i: (i, 0)),
        out_shape=jax.ShapeDtypeStruct((m, d), BF16),
        compiler_params=_params("parallel"),
        name="prenorm",
    )(x2, g)


def _postmix_kernel(y_ref, x_ref, g1_ref, g2_ref, x1_ref, h2_ref):
    x1 = x_ref[...] + _rms(y_ref[...], g1_ref[...])
    x1_ref[...] = x1
    h2_ref[...] = _rms(x1, g2_ref[...]).astype(h2_ref.dtype)


def _postmix(y, x2, g_post, g_pre):
    m, d = x2.shape
    row = pl.BlockSpec((NORM_ROWS, d), lambda i: (i, 0))
    vec = pl.BlockSpec((1, d), lambda i: (0, 0))
    return pl.pallas_call(
        _postmix_kernel,
        grid=(m // NORM_ROWS,),
        in_specs=[row, row, vec, vec],
        out_specs=[row, row],
        out_shape=[jax.ShapeDtypeStruct((m, d), F32), jax.ShapeDtypeStruct((m, d), BF16)],
        compiler_params=_params("parallel"),
        name="postmix",
    )(y, x2, g_post, g_pre)


ROPE_COLS = 256


def _qkv_kernel(h_ref, w_ref, pos_ref, ws1_ref, ws2_ref, o_ref, wd1_ref, wd2_ref, cos_ref, sup_ref, sdn_ref,
                *, n_rope_tiles):
    j = pl.program_id(1)
    half = ROPE_DIM // 2
    wd1_ref[...] = ws1_ref[...].astype(wd1_ref.dtype)
    wd2_ref[...] = ws2_ref[...].astype(wd2_ref.dtype)

    @pl.when(j == 0)
    def _():
        lane = lax.broadcasted_iota(jnp.int32, (1, HEAD_DIM), 1)
        fidx = (lane % half).astype(F32)
        inv_freq = jnp.exp(fidx * (-2.0 * math.log(ROPE_THETA) / ROPE_DIM))
        ang = pos_ref[...].astype(F32) * inv_freq
        c = jnp.cos(ang)
        s = jnp.sin(ang)
        in_rope = lane < ROPE_DIM
        cos_ref[...] = jnp.where(in_rope, c, 1.0)
        sup_ref[...] = jnp.where(lane < half, -s, 0.0)
        sdn_ref[...] = jnp.where((lane >= half) & in_rope, s, 0.0)

    @pl.when(j < n_rope_tiles)
    def _():
        h = h_ref[...]
        for c0 in range(0, o_ref.shape[1], ROPE_COLS):
            acc = jnp.dot(h, w_ref[:, c0:c0 + ROPE_COLS], preferred_element_type=F32)
            for hh in range(ROPE_COLS // HEAD_DIM):
                t = acc[:, hh * HEAD_DIM:(hh + 1) * HEAD_DIM]
                up = pltpu.roll(t, HEAD_DIM - half, axis=1)
                dn = pltpu.roll(t, half, axis=1)
                r = t * cos_ref[...] + up * sup_ref[...] + dn * sdn_ref[...]
                o_ref[:, c0 + hh * HEAD_DIM:c0 + (hh + 1) * HEAD_DIM] = r.astype(o_ref.dtype)

    @pl.when(j >= n_rope_tiles)
    def _():
        o_ref[...] = jnp.dot(h_ref[...], w_ref[...], preferred_element_type=F32).astype(o_ref.dtype)


def _qkv_proj(h, w_in_b, pos2, attn_width, kv_width, w_side1, w_side2):
    m, d = h.shape
    n = attn_width + 2 * kv_width
    assert attn_width % COL_TILE == 0 and kv_width % COL_TILE == 0 and COL_TILE % HEAD_DIM == 0
    n_rope_tiles = (attn_width + kv_width) // COL_TILE
    nj = n // COL_TILE
    step = lambda i, j: i * nj + j
    s1_spec, s1_shape = _cast_spec(w_side1, (m // ROW_TILE) * nj, step)
    s2_spec, s2_shape = _cast_spec(w_side2, (m // ROW_TILE) * nj, step)
    return pl.pallas_call(
        functools.partial(_qkv_kernel, n_rope_tiles=n_rope_tiles),
        grid=(m // ROW_TILE, nj),
        in_specs=[pl.BlockSpec((ROW_TILE, d), lambda i, j: (i, 0)),
                  pl.BlockSpec((d, COL_TILE), lambda i, j: (0, j)),
                  pl.BlockSpec((ROW_TILE, 1), lambda i, j: (i, 0)),
                  s1_spec, s2_spec],
        out_specs=[pl.BlockSpec((ROW_TILE, COL_TILE), lambda i, j: (i, j)), s1_spec, s2_spec],
        out_shape=[jax.ShapeDtypeStruct((m, n), BF16), s1_shape, s2_shape],
        scratch_shapes=[pltpu.VMEM((ROW_TILE, HEAD_DIM), F32)] * 3,
        compiler_params=_params("arbitrary", "arbitrary"),
        name="qkv_proj",
    )(h, w_in_b, pos2, w_side1, w_side2)


GLU_COLS = 256


def _attn_glu_kernel(sink_ref, q_ref, k0_ref, k1_ref, k2_ref, v0_ref, v1_ref, v2_ref, g_ref,
                     h_ref, wa_ref, wb_ref, wsg_ref, wsu_ref,
                     o_ref, u_ref, wgu_ref, acc_ref, bias_ref, *, n_kv, q_per_kv, nb):
    ft = wgu_ref.shape[2] // 2
    for t in range(wgu_ref.shape[0]):
        wgu_ref[t, :, 0:ft] = wsg_ref[:, t * ft:(t + 1) * ft].astype(wgu_ref.dtype)
        wgu_ref[t, :, ft:2 * ft] = wsu_ref[:, t * ft:(t + 1) * ft].astype(wgu_ref.dtype)

    glu_rows = h_ref.shape[0] // n_kv

    n = lax.rem(pl.program_id(0) * pl.num_programs(1) + pl.program_id(1), nb)
    span = 3 * Q_TILE
    r = lax.broadcasted_iota(jnp.int32, (Q_TILE, span), 0)
    c = lax.broadcasted_iota(jnp.int32, (Q_TILE, span), 1)
    lo = jnp.where(n == 0, Q_TILE, 0)
    hi = jnp.where(n == nb - 1, 2 * Q_TILE, span)
    rel = c - Q_TILE - r
    valid = (rel >= -WINDOW) & (rel <= WINDOW) & (c >= lo) & (c < hi)
    bias_ref[...] = jnp.where(valid, 0.0, -jnp.inf)
    log2e = math.log2(math.e)
    scale2 = HEAD_DIM ** -0.5 * log2e
    def scores(h):
        hs = slice(h * HEAD_DIM, (h + 1) * HEAD_DIM)
        kh = jnp.concatenate([k0_ref[0, :, hs], k1_ref[0, :, hs], k2_ref[0, :, hs]], axis=0)
        qh = jnp.concatenate(
            [q_ref[0, :, (h * q_per_kv + g) * HEAD_DIM:(h * q_per_kv + g + 1) * HEAD_DIM]
             for g in range(q_per_kv)], axis=0)
        return lax.dot_general(qh, kh, (((1,), (1,)), ((), ())), preferred_element_type=F32)

    s_next = scores(0)
    for h in range(n_kv):
        s = s_next
        if h + 1 < n_kv:
            s_next = scores(h + 1)
        rs = slice(h * glu_rows, (h + 1) * glu_rows)
        h_in = h_ref[rs, :]
        ga = jnp.dot(h_in, wa_ref[...], preferred_element_type=F32)
        gb = jnp.dot(h_in, wb_ref[...], preferred_element_type=F32)
        u_ref[rs, :] = (ga * jax.nn.sigmoid(gb)).astype(u_ref.dtype)
        hs = slice(h * HEAD_DIM, (h + 1) * HEAD_DIM)
        vh = jnp.concatenate([v0_ref[0, :, hs], v1_ref[0, :, hs], v2_ref[0, :, hs]], axis=0)
        for g in range(q_per_kv):
            sink2 = sink_ref[h * q_per_kv + g] * log2e
            sg = s[g * Q_TILE:(g + 1) * Q_TILE, :] * scale2 + bias_ref[...]
            mx = jnp.maximum(jnp.max(sg, axis=-1, keepdims=True), sink2)
            p = jnp.exp2(sg - mx)
            denom = jnp.sum(p, axis=-1, keepdims=True) + jnp.exp2(sink2 - mx)
            pn = (p / denom).astype(vh.dtype)
            col = (h * q_per_kv + g) * HEAD_DIM
            acc_ref[:, col:col + HEAD_DIM] = jnp.dot(pn, vh, preferred_element_type=F32)
    o_ref[0] = _rms(acc_ref[...], g_ref[...]).astype(o_ref.dtype)


def _attn_glu(qkv3, sinks, g, attn_width, kv_width, h, w, a_off, b_off, n_out, w_gate, w_up):
    b, s, _ = qkv3.shape
    m, d = h.shape
    nb = s // Q_TILE
    assert WINDOW == Q_TILE and s % Q_TILE == 0 and attn_width % kv_width == 0
    assert a_off % GLU_COLS == 0 and b_off % GLU_COLS == 0 and n_out % GLU_COLS == 0
    ni, nj = m // ROW_TILE, n_out // GLU_COLS
    assert ni * nj == b * nb
    n_kv = kv_width // HEAD_DIM
    q_per_kv = attn_width // kv_width
    kcol = attn_width // kv_width
    vcol = kcol + 1
    ao, bo = a_off // GLU_COLS, b_off // GLU_COLS
    step = lambda i, j: i * nj + j

    def blk(d_blk):
        return lambda i, j: (step(i, j) // nb, jnp.clip(step(i, j) % nb + d_blk, 0, nb - 1))

    def kv_spec(col, d_blk):
        return pl.BlockSpec((1, Q_TILE, kv_width), lambda i, j: (*blk(d_blk)(i, j), col))

    q_spec = pl.BlockSpec((1, Q_TILE, attn_width), lambda i, j: (*blk(0)(i, j), 0))
    assert w_gate.shape == w_up.shape and w_gate.shape[1] % FF_TILE == 0
    d_ff = w_gate.shape[1]
    s1_spec, _ = _cast_spec(w_gate, ni * nj, step)
    s2_spec = s1_spec
    slab = s1_spec.block_shape[0]
    last_slab = w_gate.shape[0] // slab - 1
    wgu_spec = pl.BlockSpec((d_ff // FF_TILE, slab, 2 * FF_TILE),
                            lambda i, j: (0, jnp.minimum(step(i, j), last_slab), 0))
    wgu_shape = jax.ShapeDtypeStruct((d_ff // FF_TILE, w_gate.shape[0], 2 * FF_TILE), BF16)
    return pl.pallas_call(
        functools.partial(_attn_glu_kernel, n_kv=n_kv, q_per_kv=q_per_kv, nb=nb),
        grid=(ni, nj),
        in_specs=[pl.BlockSpec(memory_space=pltpu.SMEM),
                  q_spec,
                  kv_spec(kcol, -1), kv_spec(kcol, 0), kv_spec(kcol, 1),
                  kv_spec(vcol, -1), kv_spec(vcol, 0), kv_spec(vcol, 1),
                  pl.BlockSpec((1, attn_width), lambda i, j: (0, 0)),
                  pl.BlockSpec((ROW_TILE, d), lambda i, j: (i, 0)),
                  pl.BlockSpec((d, GLU_COLS), lambda i, j: (0, ao + j)),
                  pl.BlockSpec((d, GLU_COLS), lambda i, j: (0, bo + j)),
                  s1_spec, s2_spec],
        out_specs=[q_spec, pl.BlockSpec((ROW_TILE, GLU_COLS), lambda i, j: (i, j)), wgu_spec],
        out_shape=[jax.ShapeDtypeStruct((b, s, attn_width), BF16), jax.ShapeDtypeStruct((m, n_out), F32),
                   wgu_shape],
        scratch_shapes=[pltpu.VMEM((Q_TILE, attn_width), F32), pltpu.VMEM((Q_TILE, 3 * Q_TILE), F32)],
        compiler_params=_params("arbitrary", "arbitrary"),
        name="attn_glu",
    )(sinks, qkv3, qkv3, qkv3, qkv3, qkv3, qkv3, qkv3, g, h, w, w, w_gate, w_up)


CONV_RC = 16
CONV_SUMS = 2
CONV_CC = 128
SUBLANES = 8


def _conv_fill_slab(t, nt, up_ref, u_ref, un_ref, slab_ref):
    rows = u_ref.shape[1]
    halo = CONV_HALO
    slab_ref[0:halo, :] = jnp.where(t == 0, 0.0, up_ref[0])
    slab_ref[halo:halo + rows, :] = u_ref[0]
    slab_ref[halo + rows:halo + rows + halo, :] = jnp.where(t == nt - 1, 0.0, un_ref[0])


def _exact_zero_of(v):
    bits = pltpu.bitcast(v, jnp.uint32)
    return pltpu.bitcast((bits >> 16) >> 16, F32)


def _conv_chunk(r0, c0, wb_ref, cb_ref, slab_ref, y_ref, ksize, after=None):
    cs = slice(c0, c0 + CONV_CC)
    off = CONV_HALO - ksize // 2
    span = CONV_RC + 2 * CONV_HALO
    raw = slab_ref[r0:r0 + span, cs]
    if after is None:
        zero = jnp.zeros((CONV_RC, CONV_CC), F32)
    else:
        zero = jnp.concatenate([_exact_zero_of(after)] * (CONV_RC // SUBLANES), axis=0)
    accs = [zero] * CONV_SUMS
    n = 0
    for sft in range(SUBLANES):
        sh = raw if sft == 0 else pltpu.roll(raw, span - sft, axis=0)
        for q in range(span // SUBLANES):
            k = q * SUBLANES + sft - off
            if 0 <= k < ksize:
                wk = jnp.concatenate([wb_ref[k, :, cs]] * (CONV_RC // SUBLANES), axis=0)
                accs[n % CONV_SUMS] = accs[n % CONV_SUMS] + sh[q * SUBLANES:q * SUBLANES + CONV_RC, :] * wk
                n += 1
    out = functools.reduce(lambda a, b: a + b, accs) + cb_ref[:, cs]
    y_ref[r0:r0 + CONV_RC, cs] = out
    return out[0:SUBLANES, :]


def _conv_norms(y_ref, lg_ref, lb_ref, og_ref):
    y = y_ref[...]
    mu = jnp.mean(y, axis=-1, keepdims=True)
    d = y - mu
    var = jnp.mean(d * d, axis=-1, keepdims=True)
    z = d * lax.rsqrt(var + LN_EPS) * lg_ref[...] + lb_ref[...]
    sw = z * jax.nn.sigmoid(z)
    return _rms(sw, og_ref[...])


OUT_COLS = 512
MXU_PIECES = 4


def _conv_out_kernel(a1_ref, w1_ref, w2_ref, up_ref, u_ref, un_ref, wb_ref, cb_ref, lg_ref, lb_ref, og_ref,
                     y_out_ref, convn_ref, slab_ref, ycv_ref, *, ksize, nt):
    i = pl.program_id(0)
    j = pl.program_id(1)
    ni = pl.num_programs(0) - 1
    nj = pl.num_programs(1)
    slot = lax.rem(i, 2)
    rows, ch = ycv_ref.shape
    chunks = [(r0, c0) for c0 in range(0, ch, CONV_CC) for r0 in range(0, rows, CONV_RC)]
    pc = y_out_ref.shape[1] // (MXU_PIECES // 2)
    per = -(-len(chunks) // MXU_PIECES)

    def step(do_mm, do_conv):
        if do_conv:
            t = lax.rem(i * nj + j, nt)
            _conv_fill_slab(t, nt, up_ref, u_ref, un_ref, slab_ref)
        if do_mm:
            a1 = a1_ref[...]
            a2 = convn_ref[1 - slot]
        dep = None
        for p in range(MXU_PIECES):
            if do_mm:
                c = (p // 2) * pc
                lhs, w_ref = (a1, w1_ref) if p % 2 == 0 else (a2, w2_ref)
                part = jnp.dot(lhs, w_ref[:, c:c + pc], preferred_element_type=F32)
                if p % 2 == 0:
                    y_out_ref[:, c:c + pc] = part
                else:
                    y_out_ref[:, c:c + pc] += part
            if do_conv:
                for r0, c0 in chunks[p * per:(p + 1) * per]:
                    out = _conv_chunk(r0, c0, wb_ref, cb_ref, slab_ref, ycv_ref, ksize, after=dep)
                    dep = out if do_mm else None
        if do_conv:
            cn = _conv_norms(ycv_ref, lg_ref, lb_ref, og_ref)
            convn_ref[slot, pl.ds(pl.multiple_of(j * rows, rows), rows), :] = cn.astype(convn_ref.dtype)

    @pl.when(i == 0)
    def _():
        y_out_ref[...] = jnp.zeros_like(y_out_ref)
        step(False, True)

    @pl.when((i > 0) & (i < ni))
    def _():
        step(True, True)

    @pl.when(i == ni)
    def _():
        step(True, False)


def _conv_outproj(a1, u3, conv_w, conv_b, ln_g, ln_b, out_g, w_b):
    m, k1 = a1.shape
    b, s, ch = u3.shape
    n = w_b.shape[1]
    ksize = conv_w.shape[0]
    ni, nj = m // ROW_TILE, n // OUT_COLS
    assert w_b.shape[0] == k1 + ch and k1 == ch and ROW_TILE % nj == 0
    rows = ROW_TILE // nj
    assert ksize // 2 <= CONV_HALO and s % rows == 0 and rows % CONV_RC == 0 and ch % CONV_CC == 0
    nt = s // rows
    w_bcast = jnp.broadcast_to(conv_w[:, None, :], (ksize, SUBLANES, ch))
    hb = rows // CONV_HALO
    last = s // CONV_HALO - 1
    prev = lambda i: jnp.maximum(i - 1, 0)

    def cblk(i, j):
        g = jnp.minimum(i, ni - 1) * nj + j
        return g // nt, g % nt

    vec = pl.BlockSpec((1, ch), lambda i, j: (0, 0))
    return pl.pallas_call(
        functools.partial(_conv_out_kernel, ksize=ksize, nt=nt),
        grid=(ni + 1, nj),
        in_specs=[pl.BlockSpec((ROW_TILE, k1), lambda i, j: (prev(i), 0)),
                  pl.BlockSpec((k1, OUT_COLS), lambda i, j: (0, j)),
                  pl.BlockSpec((ch, OUT_COLS), lambda i, j: (1, j)),
                  pl.BlockSpec((1, CONV_HALO, ch),
                               lambda i, j: (cblk(i, j)[0], jnp.maximum(cblk(i, j)[1] * hb - 1, 0), 0)),
                  pl.BlockSpec((1, rows, ch), lambda i, j: (*cblk(i, j), 0)),
                  pl.BlockSpec((1, CONV_HALO, ch),
                               lambda i, j: (cblk(i, j)[0], jnp.minimum((cblk(i, j)[1] + 1) * hb, last), 0)),
                  pl.BlockSpec((ksize, SUBLANES, ch), lambda i, j: (0, 0, 0)),
                  vec, vec, vec, vec],
        out_specs=pl.BlockSpec((ROW_TILE, OUT_COLS), lambda i, j: (prev(i), jnp.where(i > 0, j, 0))),
        out_shape=jax.ShapeDtypeStruct((m, n), F32),
        scratch_shapes=[pltpu.VMEM((2, ROW_TILE, ch), BF16),
                        pltpu.VMEM((rows + 2 * CONV_HALO, ch), F32),
                        pltpu.VMEM((rows, ch), F32)],
        compiler_params=_params("arbitrary", "arbitrary"),
        name="conv_outproj",
    )(a1, w_b, w_b, u3, u3, u3, w_bcast, conv_b, ln_g, ln_b, out_g)


FF_TILE = 256
FIN_TILE = 512
DOWN_CHUNK = 512


def _ffn_kernel(h_ref, wgu_ref, wd_ref, x1_ref, g_ref, o_ref, acc_ref, scale_ref, *, n_f):
    j = pl.program_id(1)
    d_model = acc_ref.shape[1]

    @pl.when((pl.program_id(0) == 0) & (j == 0))
    def _():
        acc_ref[...] = jnp.zeros_like(acc_ref)

    @pl.when(j < n_f)
    def _():
        h = h_ref[...]
        gate = jnp.dot(h, wgu_ref[0, :, 0:FF_TILE], preferred_element_type=F32)
        up = jnp.dot(h, wgu_ref[0, :, FF_TILE:2 * FF_TILE], preferred_element_type=F32)
        a = ((gate * jax.nn.sigmoid(gate)) * up).astype(h.dtype)
        for n0 in range(0, d_model, DOWN_CHUNK):
            acc_ref[:, n0:n0 + DOWN_CHUNK] += jnp.dot(a, wd_ref[:, n0:n0 + DOWN_CHUNK],
                                                       preferred_element_type=F32)

    @pl.when(j == n_f)
    def _():
        y = acc_ref[...]
        scale_ref[...] = lax.rsqrt(jnp.mean(y * y, axis=-1, keepdims=True) + RMS_EPS)

    @pl.when(j >= n_f)
    def _():
        cs = pl.ds(pl.multiple_of((j - n_f) * FIN_TILE, FIN_TILE), FIN_TILE)
        o_ref[...] = x1_ref[...] + acc_ref[:, cs] * scale_ref[...] * g_ref[...]
        acc_ref[:, cs] = jnp.zeros((acc_ref.shape[0], FIN_TILE), F32)


def _ffn(h, wgu_b, w_down_b, x1, g):
    m, d = h.shape
    n_f = wgu_b.shape[0]
    assert wgu_b.shape[1:] == (d, 2 * FF_TILE) and w_down_b.shape == (n_f * FF_TILE, d)
    assert d % FIN_TILE == 0 and d % DOWN_CHUNK == 0
    n_fin = d // FIN_TILE
    ftile = lambda i, j: jnp.minimum(j, n_f - 1)
    fin = lambda i, j: jnp.maximum(j - n_f, 0)
    return pl.pallas_call(
        functools.partial(_ffn_kernel, n_f=n_f),
        grid=(m // ROW_TILE, n_f + n_fin),
        in_specs=[pl.BlockSpec((ROW_TILE, d), lambda i, j: (i, 0)),
                  pl.BlockSpec((1, d, 2 * FF_TILE), lambda i, j: (ftile(i, j), 0, 0)),
                  pl.BlockSpec((FF_TILE, d), lambda i, j: (ftile(i, j), 0)),
                  pl.BlockSpec((ROW_TILE, FIN_TILE), lambda i, j: (i, fin(i, j))),
                  pl.BlockSpec((1, FIN_TILE), lambda i, j: (0, fin(i, j)))],
        out_specs=pl.BlockSpec((ROW_TILE, FIN_TILE), lambda i, j: (i, fin(i, j))),
        out_shape=jax.ShapeDtypeStruct((m, d), F32),
        scratch_shapes=[pltpu.VMEM((ROW_TILE, d), F32), pltpu.VMEM((ROW_TILE, 1), F32)],
        compiler_params=_params("arbitrary", "arbitrary"),
        name="ffn",
    )(h, wgu_b, w_down_b, x1, g)


def kernel(x, positions, mix_pre_g, w_in, sinks, conv_w, conv_b, conv_ln_g, conv_ln_b, attn_out_g,
           conv_out_g, w_out, mix_post_g, ffn_pre_g, w_gate, w_up, w_down, ffn_post_g):
    b, s, d = x.shape
    depth = w_in.shape[0]
    attn_width = attn_out_g.shape[1]
    conv_width = conv_out_g.shape[1]
    kv_width = (w_in.shape[2] - attn_width - 2 * conv_width) // 2
    m = b * s
    assert m % ROW_TILE == 0 and m % NORM_ROWS == 0

    x2 = x.reshape(m, d)
    pos2 = positions.reshape(m, 1)
    row = lambda v: v.reshape(1, -1)

    for l in range(depth):
        w_in_b = w_in[l].astype(BF16)

        h1 = _prenorm(x2, row(mix_pre_g[l]))
        qkv, w_down_b, w_out_b = _qkv_proj(h1, w_in_b, pos2, attn_width, kv_width, w_down[l], w_out[l])
        qkv_n = attn_width + 2 * kv_width
        attn_n, u, wgu_b = _attn_glu(
            qkv.reshape(b, s, qkv_n), sinks[l], row(attn_out_g[l]), attn_width, kv_width,
            h1, w_in_b, qkv_n, qkv_n + conv_width, conv_width, w_gate[l], w_up[l])
        y = _conv_outproj(attn_n.reshape(m, attn_width), u.reshape(b, s, conv_width), conv_w[l],
                          row(conv_b[l]), row(conv_ln_g[l]), row(conv_ln_b[l]), row(conv_out_g[l]), w_out_b)
        x1, h2 = _postmix(y, x2, row(mix_post_g[l]), row(ffn_pre_g[l]))

        x2 = _ffn(h2, wgu_b, w_down_b, x1, row(ffn_post_g[l]))
    return x2.reshape(b, s, d)
```

```python
import functools
import math

import jax
import jax.numpy as jnp
from jax import lax
from jax.experimental import pallas as pl
from jax.experimental.pallas import tpu as pltpu

F32 = jnp.float32
BF16 = jnp.bfloat16

HEAD_DIM = 128
ROPE_DIM = HEAD_DIM // 4
ROPE_THETA = 500000.0
WINDOW = 128
RMS_EPS = 1e-6
LN_EPS = 1e-5

V7X_VMEM_BYTES = 64 * 1024 * 1024
VMEM_LIMIT = V7X_VMEM_BYTES - 6 * 1024 * 1024

ROW_TILE = 1024
COL_TILE = 512
NORM_ROWS = 256
Q_TILE = 128
CONV_HALO = 16


def _params(*sem):
    return pltpu.CompilerParams(dimension_semantics=sem, vmem_limit_bytes=VMEM_LIMIT)


def _rms(t, g):
    ms = jnp.mean(t * t, axis=-1, keepdims=True)
    return t * lax.rsqrt(ms + RMS_EPS) * g


BF16_ROWS = 16


def _cast_spec(w, n_steps, step):
    r, c = w.shape
    rows = next(d for d in range(BF16_ROWS, r + 1, BF16_ROWS) if r % d == 0 and d * n_steps >= r)
    last = r // rows - 1
    spec = pl.BlockSpec((rows, c), lambda *g: (jnp.minimum(step(*g), last), 0))
    return spec, jax.ShapeDtypeStruct((r, c), BF16)


def _prenorm_kernel(x_ref, g_ref, pos_ref, h_ref, cos_ref, sup_ref, sdn_ref):
    h_ref[...] = _rms(x_ref[...], g_ref[...]).astype(h_ref.dtype)
    half = ROPE_DIM // 2
    lane = lax.broadcasted_iota(jnp.int32, (1, HEAD_DIM), 1)
    fidx = (lane % half).astype(F32)
    inv_freq = jnp.exp(fidx * (-2.0 * math.log(ROPE_THETA) / ROPE_DIM))
    ang = pos_ref[...].astype(F32) * inv_freq
    c = jnp.cos(ang)
    s = jnp.sin(ang)
    in_rope = lane < ROPE_DIM
    cos_ref[...] = jnp.where(in_rope, c, 1.0)
    sup_ref[...] = jnp.where(lane < half, -s, 0.0)
    sdn_ref[...] = jnp.where((lane >= half) & in_rope, s, 0.0)


def _prenorm(x2, g, pos2):
    m, d = x2.shape
    tab = pl.BlockSpec((NORM_ROWS, HEAD_DIM), lambda i: (i, 0))
    tab_shape = jax.ShapeDtypeStruct((m, HEAD_DIM), F32)
    return pl.pallas_call(
        _prenorm_kernel,
        grid=(m // NORM_ROWS,),
        in_specs=[pl.BlockSpec((NORM_ROWS, d), lambda i: (i, 0)),
                  pl.BlockSpec((1, d), lambda i: (0, 0)),
                  pl.BlockSpec((NORM_ROWS, 1), lambda i: (i, 0))],
        out_specs=[pl.BlockSpec((NORM_ROWS, d), lambda i: (i, 0)), tab, tab, tab],
        out_shape=[jax.ShapeDtypeStruct((m, d), BF16), tab_shape, tab_shape, tab_shape],
        compiler_params=_params("parallel"),
        name="prenorm",
    )(x2, g, pos2)


def _postmix_kernel(y_ref, x_ref, g1_ref, g2_ref, x1_ref, h2_ref):
    x1 = x_ref[...] + _rms(y_ref[...], g1_ref[...])
    x1_ref[...] = x1
    h2_ref[...] = _rms(x1, g2_ref[...]).astype(h2_ref.dtype)


def _postmix(y, x2, g_post, g_pre):
    m, d = x2.shape
    row = pl.BlockSpec((NORM_ROWS, d), lambda i: (i, 0))
    vec = pl.BlockSpec((1, d), lambda i: (0, 0))
    return pl.pallas_call(
        _postmix_kernel,
        grid=(m // NORM_ROWS,),
        in_specs=[row, row, vec, vec],
        out_specs=[row, row],
        out_shape=[jax.ShapeDtypeStruct((m, d), F32), jax.ShapeDtypeStruct((m, d), BF16)],
        compiler_params=_params("parallel"),
        name="postmix",
    )(y, x2, g_post, g_pre)


ROPE_COLS = 256


def _qkv_kernel(h_ref, w_ref, cos_ref, sup_ref, sdn_ref, wsrc_ref, o_ref, wdst_ref, *, n_rope_tiles):
    j = pl.program_id(1)
    half = ROPE_DIM // 2
    wdst_ref[...] = wsrc_ref[...].astype(wdst_ref.dtype)

    @pl.when(j < n_rope_tiles)
    def _():
        h = h_ref[...]
        for c0 in range(0, o_ref.shape[1], ROPE_COLS):
            acc = jnp.dot(h, w_ref[:, c0:c0 + ROPE_COLS], preferred_element_type=F32)
            for hh in range(ROPE_COLS // HEAD_DIM):
                t = acc[:, hh * HEAD_DIM:(hh + 1) * HEAD_DIM]
                up = pltpu.roll(t, HEAD_DIM - half, axis=1)
                dn = pltpu.roll(t, half, axis=1)
                r = t * cos_ref[...] + up * sup_ref[...] + dn * sdn_ref[...]
                o_ref[:, c0 + hh * HEAD_DIM:c0 + (hh + 1) * HEAD_DIM] = r.astype(o_ref.dtype)

    @pl.when(j >= n_rope_tiles)
    def _():
        o_ref[...] = jnp.dot(h_ref[...], w_ref[...], preferred_element_type=F32).astype(o_ref.dtype)


def _qkv_proj(h, w_in_b, rope_tabs, attn_width, kv_width, w_side):
    m, d = h.shape
    n = attn_width + 2 * kv_width
    assert attn_width % COL_TILE == 0 and kv_width % COL_TILE == 0 and COL_TILE % HEAD_DIM == 0
    n_rope_tiles = (attn_width + kv_width) // COL_TILE
    nj = n // COL_TILE
    side_spec, side_shape = _cast_spec(w_side, (m // ROW_TILE) * nj, lambda i, j: i * nj + j)
    tab = pl.BlockSpec((ROW_TILE, HEAD_DIM), lambda i, j: (i, 0))
    return pl.pallas_call(
        functools.partial(_qkv_kernel, n_rope_tiles=n_rope_tiles),
        grid=(m // ROW_TILE, nj),
        in_specs=[pl.BlockSpec((ROW_TILE, d), lambda i, j: (i, 0)),
                  pl.BlockSpec((d, COL_TILE), lambda i, j: (0, j)),
                  tab, tab, tab,
                  side_spec],
        out_specs=[pl.BlockSpec((ROW_TILE, COL_TILE), lambda i, j: (i, j)), side_spec],
        out_shape=[jax.ShapeDtypeStruct((m, n), BF16), side_shape],
        compiler_params=_params("arbitrary", "arbitrary"),
        name="qkv_proj",
    )(h, w_in_b, *rope_tabs, w_side)


GLU_COLS = 256


def _attn_glu_kernel(sink_ref, q_ref, k0_ref, k1_ref, k2_ref, v0_ref, v1_ref, v2_ref, g_ref,
                     h_ref, wa_ref, wb_ref, ws1_ref, ws2_ref,
                     o_ref, u_ref, wd1_ref, wd2_ref, acc_ref, bias_ref, *, n_kv, q_per_kv, nb):
    wd1_ref[...] = ws1_ref[...].astype(wd1_ref.dtype)
    wd2_ref[...] = ws2_ref[...].astype(wd2_ref.dtype)

    glu_rows = h_ref.shape[0] // n_kv

    step_id = pl.program_id(0) * pl.num_programs(1) + pl.program_id(1)
    n = lax.rem(step_id, nb)
    span = 3 * Q_TILE

    @pl.when(step_id == 0)
    def _():
        r = lax.broadcasted_iota(jnp.int32, (Q_TILE, span), 0)
        c = lax.broadcasted_iota(jnp.int32, (Q_TILE, span), 1)
        rel = c - Q_TILE - r
        band = (rel >= -WINDOW) & (rel <= WINDOW)
        for kind, (lo, hi) in enumerate(((Q_TILE, span), (0, span), (0, 2 * Q_TILE))):
            bias_ref[kind] = jnp.where(band & (c >= lo) & (c < hi), 0.0, -jnp.inf)

    bias_kind = jnp.where(n == 0, 0, jnp.where(n == nb - 1, 2, 1))
    log2e = math.log2(math.e)
    scale2 = HEAD_DIM ** -0.5 * log2e

    def scores(h):
        hs = slice(h * HEAD_DIM, (h + 1) * HEAD_DIM)
        kh = jnp.concatenate([k0_ref[0, :, hs], k1_ref[0, :, hs], k2_ref[0, :, hs]], axis=0)
        qh = jnp.concatenate(
            [q_ref[0, :, (h * q_per_kv + g) * HEAD_DIM:(h * q_per_kv + g + 1) * HEAD_DIM]
             for g in range(q_per_kv)], axis=0)
        return lax.dot_general(qh, kh, (((1,), (1,)), ((), ())), preferred_element_type=F32)

    s_next = scores(0)
    for h in range(n_kv):
        s = s_next
        if h + 1 < n_kv:
            s_next = scores(h + 1)
        rs = slice(h * glu_rows, (h + 1) * glu_rows)
        h_in = h_ref[rs, :]
        ga = jnp.dot(h_in, wa_ref[...], preferred_element_type=F32)
        gb = jnp.dot(h_in, wb_ref[...], preferred_element_type=F32)
        u_ref[rs, :] = (ga * jax.nn.sigmoid(gb)).astype(u_ref.dtype)
        hs = slice(h * HEAD_DIM, (h + 1) * HEAD_DIM)
        vh = jnp.concatenate([v0_ref[0, :, hs], v1_ref[0, :, hs], v2_ref[0, :, hs]], axis=0)
        for g in range(q_per_kv):
            sink2 = sink_ref[h * q_per_kv + g] * log2e
            sg = s[g * Q_TILE:(g + 1) * Q_TILE, :] * scale2 + bias_ref[bias_kind]
            mx = jnp.maximum(jnp.max(sg, axis=-1, keepdims=True), sink2)
            p = jnp.exp2(sg - mx)
            denom = jnp.sum(p, axis=-1, keepdims=True) + jnp.exp2(sink2 - mx)
            pn = (p / denom).astype(vh.dtype)
            col = (h * q_per_kv + g) * HEAD_DIM
            acc_ref[:, col:col + HEAD_DIM] = jnp.dot(pn, vh, preferred_element_type=F32)
    o_ref[0] = _rms(acc_ref[...], g_ref[...]).astype(o_ref.dtype)


def _attn_glu(qkv3, sinks, g, attn_width, kv_width, h, w, a_off, b_off, n_out, w_side1, w_side2):
    b, s, _ = qkv3.shape
    m, d = h.shape
    nb = s // Q_TILE
    assert WINDOW == Q_TILE and s % Q_TILE == 0 and nb >= 2 and attn_width % kv_width == 0
    assert a_off % GLU_COLS == 0 and b_off % GLU_COLS == 0 and n_out % GLU_COLS == 0
    ni, nj = m // ROW_TILE, n_out // GLU_COLS
    assert ni * nj == b * nb
    n_kv = kv_width // HEAD_DIM
    q_per_kv = attn_width // kv_width
    kcol = attn_width // kv_width
    vcol = kcol + 1
    ao, bo = a_off // GLU_COLS, b_off // GLU_COLS
    step = lambda i, j: i * nj + j

    def blk(d_blk):
        return lambda i, j: (step(i, j) // nb, jnp.clip(step(i, j) % nb + d_blk, 0, nb - 1))

    def kv_spec(col, d_blk):
        return pl.BlockSpec((1, Q_TILE, kv_width), lambda i, j: (*blk(d_blk)(i, j), col))

    q_spec = pl.BlockSpec((1, Q_TILE, attn_width), lambda i, j: (*blk(0)(i, j), 0))
    s1_spec, s1_shape = _cast_spec(w_side1, ni * nj, step)
    s2_spec, s2_shape = _cast_spec(w_side2, ni * nj, step)
    return pl.pallas_call(
        functools.partial(_attn_glu_kernel, n_kv=n_kv, q_per_kv=q_per_kv, nb=nb),
        grid=(ni, nj),
        in_specs=[pl.BlockSpec(memory_space=pltpu.SMEM),
                  q_spec,
                  kv_spec(kcol, -1), kv_spec(kcol, 0), kv_spec(kcol, 1),
                  kv_spec(vcol, -1), kv_spec(vcol, 0), kv_spec(vcol, 1),
                  pl.BlockSpec((1, attn_width), lambda i, j: (0, 0)),
                  pl.BlockSpec((ROW_TILE, d), lambda i, j: (i, 0)),
                  pl.BlockSpec((d, GLU_COLS), lambda i, j: (0, ao + j)),
                  pl.BlockSpec((d, GLU_COLS), lambda i, j: (0, bo + j)),
                  s1_spec, s2_spec],
        out_specs=[q_spec, pl.BlockSpec((ROW_TILE, GLU_COLS), lambda i, j: (i, j)), s1_spec, s2_spec],
        out_shape=[jax.ShapeDtypeStruct((b, s, attn_width), BF16), jax.ShapeDtypeStruct((m, n_out), F32),
                   s1_shape, s2_shape],
        scratch_shapes=[pltpu.VMEM((Q_TILE, attn_width), F32), pltpu.VMEM((3, Q_TILE, 3 * Q_TILE), F32)],
        compiler_params=_params("arbitrary", "arbitrary"),
        name="attn_glu",
    )(sinks, qkv3, qkv3, qkv3, qkv3, qkv3, qkv3, qkv3, g, h, w, w, w_side1, w_side2)


CONV_RC = 16
CONV_SUMS = 2
CONV_CC = 128
SUBLANES = 8


def _conv_fill_slab(t, nt, up_ref, u_ref, un_ref, slab_ref):
    rows = u_ref.shape[1]
    halo = CONV_HALO
    slab_ref[0:halo, :] = jnp.where(t == 0, 0.0, up_ref[0])
    slab_ref[halo:halo + rows, :] = u_ref[0]
    slab_ref[halo + rows:halo + rows + halo, :] = jnp.where(t == nt - 1, 0.0, un_ref[0])


def _exact_zero_of(v):
    bits = pltpu.bitcast(v, jnp.uint32)
    return pltpu.bitcast((bits >> 16) >> 16, F32)


def _conv_chunk(r0, c0, wb_ref, cb_ref, slab_ref, y_ref, ksize, after=None):
    cs = slice(c0, c0 + CONV_CC)
    off = CONV_HALO - ksize // 2
    span = CONV_RC + 2 * CONV_HALO
    raw = slab_ref[r0:r0 + span, cs]
    if after is None:
        zero = jnp.zeros((CONV_RC, CONV_CC), F32)
    else:
        zero = jnp.concatenate([_exact_zero_of(after)] * (CONV_RC // SUBLANES), axis=0)
    accs = [zero] * CONV_SUMS
    n = 0
    for sft in range(SUBLANES):
        sh = raw if sft == 0 else pltpu.roll(raw, span - sft, axis=0)
        for q in range(span // SUBLANES):
            k = q * SUBLANES + sft - off
            if 0 <= k < ksize:
                wk = jnp.concatenate([wb_ref[k, :, cs]] * (CONV_RC // SUBLANES), axis=0)
                accs[n % CONV_SUMS] = accs[n % CONV_SUMS] + sh[q * SUBLANES:q * SUBLANES + CONV_RC, :] * wk
                n += 1
    out = functools.reduce(lambda a, b: a + b, accs) + cb_ref[:, cs]
    y_ref[r0:r0 + CONV_RC, cs] = out
    return out[0:SUBLANES, :]


def _conv_norms(y_ref, lg_ref, lb_ref, og_ref):
    y = y_ref[...]
    mu = jnp.mean(y, axis=-1, keepdims=True)
    d = y - mu
    var = jnp.mean(d * d, axis=-1, keepdims=True)
    z = d * lax.rsqrt(var + LN_EPS) * lg_ref[...] + lb_ref[...]
    sw = z * jax.nn.sigmoid(z)
    return _rms(sw, og_ref[...])


OUT_COLS = 512
MXU_PIECES = 4


def _conv_out_kernel(a1_ref, w1_ref, w2_ref, up_ref, u_ref, un_ref, wb_ref, cb_ref, lg_ref, lb_ref, og_ref,
                     wsrc_ref, y_out_ref, wdst_ref, convn_ref, slab_ref, ycv_ref, *, ksize, nt):
    i = pl.program_id(0)
    j = pl.program_id(1)
    ni = pl.num_programs(0) - 1
    nj = pl.num_programs(1)
    wdst_ref[...] = wsrc_ref[...].astype(wdst_ref.dtype)
    slot = lax.rem(i, 2)
    rows, ch = ycv_ref.shape
    chunks = [(r0, c0) for c0 in range(0, ch, CONV_CC) for r0 in range(0, rows, CONV_RC)]
    pc = y_out_ref.shape[1] // (MXU_PIECES // 2)
    per = -(-len(chunks) // MXU_PIECES)

    def step(do_mm, do_conv):
        if do_conv:
            t = lax.rem(i * nj + j, nt)
            _conv_fill_slab(t, nt, up_ref, u_ref, un_ref, slab_ref)
        if do_mm:
            a1 = a1_ref[...]
            a2 = convn_ref[1 - slot]
        dep = None
        for p in range(MXU_PIECES):
            if do_mm:
                c = (p // 2) * pc
                lhs, w_ref = (a1, w1_ref) if p % 2 == 0 else (a2, w2_ref)
                part = jnp.dot(lhs, w_ref[:, c:c + pc], preferred_element_type=F32)
                if p % 2 == 0:
                    y_out_ref[:, c:c + pc] = part
                else:
                    y_out_ref[:, c:c + pc] += part
            if do_conv:
                for r0, c0 in chunks[p * per:(p + 1) * per]:
                    out = _conv_chunk(r0, c0, wb_ref, cb_ref, slab_ref, ycv_ref, ksize, after=dep)
                    dep = out if do_mm else None
        if do_conv:
            cn = _conv_norms(ycv_ref, lg_ref, lb_ref, og_ref)
            convn_ref[slot, pl.ds(pl.multiple_of(j * rows, rows), rows), :] = cn.astype(convn_ref.dtype)

    @pl.when(i == 0)
    def _():
        y_out_ref[...] = jnp.zeros_like(y_out_ref)
        step(False, True)

    @pl.when((i > 0) & (i < ni))
    def _():
        step(True, True)

    @pl.when(i == ni)
    def _():
        step(True, False)


def _conv_outproj(a1, u3, conv_w, conv_b, ln_g, ln_b, out_g, w_b, w_side):
    m, k1 = a1.shape
    b, s, ch = u3.shape
    n = w_b.shape[1]
    ksize = conv_w.shape[0]
    ni, nj = m // ROW_TILE, n // OUT_COLS
    assert w_b.shape[0] == k1 + ch and k1 == ch and ROW_TILE % nj == 0
    rows = ROW_TILE // nj
    assert ksize // 2 <= CONV_HALO and s % rows == 0 and rows % CONV_RC == 0 and ch % CONV_CC == 0
    nt = s // rows
    w_bcast = jnp.broadcast_to(conv_w[:, None, :], (ksize, SUBLANES, ch))
    hb = rows // CONV_HALO
    last = s // CONV_HALO - 1
    prev = lambda i: jnp.maximum(i - 1, 0)

    def cblk(i, j):
        g = jnp.minimum(i, ni - 1) * nj + j
        return g // nt, g % nt

    vec = pl.BlockSpec((1, ch), lambda i, j: (0, 0))
    side_spec, side_shape = _cast_spec(w_side, (ni + 1) * nj, lambda i, j: i * nj + j)
    return pl.pallas_call(
        functools.partial(_conv_out_kernel, ksize=ksize, nt=nt),
        grid=(ni + 1, nj),
        in_specs=[pl.BlockSpec((ROW_TILE, k1), lambda i, j: (prev(i), 0)),
                  pl.BlockSpec((k1, OUT_COLS), lambda i, j: (0, j)),
                  pl.BlockSpec((ch, OUT_COLS), lambda i, j: (1, j)),
                  pl.BlockSpec((1, CONV_HALO, ch),
                               lambda i, j: (cblk(i, j)[0], jnp.maximum(cblk(i, j)[1] * hb - 1, 0), 0)),
                  pl.BlockSpec((1, rows, ch), lambda i, j: (*cblk(i, j), 0)),
                  pl.BlockSpec((1, CONV_HALO, ch),
                               lambda i, j: (cblk(i, j)[0], jnp.minimum((cblk(i, j)[1] + 1) * hb, last), 0)),
                  pl.BlockSpec((ksize, SUBLANES, ch), lambda i, j: (0, 0, 0)),
                  vec, vec, vec, vec, side_spec],
        out_specs=[pl.BlockSpec((ROW_TILE, OUT_COLS), lambda i, j: (prev(i), jnp.where(i > 0, j, 0))),
                   side_spec],
        out_shape=[jax.ShapeDtypeStruct((m, n), F32), side_shape],
        scratch_shapes=[pltpu.VMEM((2, ROW_TILE, ch), BF16),
                        pltpu.VMEM((rows + 2 * CONV_HALO, ch), F32),
                        pltpu.VMEM((rows, ch), F32)],
        compiler_params=_params("arbitrary", "arbitrary"),
        name="conv_outproj",
    )(a1, w_b, w_b, u3, u3, u3, w_bcast, conv_b, ln_g, ln_b, out_g, w_side)


FF_TILE = 256
FIN_TILE = 512
DOWN_CHUNK = 512


def _ffn_kernel(h_ref, wg_ref, wu_ref, wd_ref, x1_ref, g_ref, o_ref, acc_ref, scale_ref, *, n_f):
    j = pl.program_id(1)
    d_model = acc_ref.shape[1]

    @pl.when((pl.program_id(0) == 0) & (j == 0))
    def _():
        acc_ref[...] = jnp.zeros_like(acc_ref)

    @pl.when(j < n_f)
    def _():
        h = h_ref[...]
        gate = jnp.dot(h, wg_ref[...], preferred_element_type=F32)
        up = jnp.dot(h, wu_ref[...], preferred_element_type=F32)
        a = ((gate * jax.nn.sigmoid(gate)) * up).astype(h.dtype)
        for n0 in range(0, d_model, DOWN_CHUNK):
            acc_ref[:, n0:n0 + DOWN_CHUNK] += jnp.dot(a, wd_ref[:, n0:n0 + DOWN_CHUNK],
                                                       preferred_element_type=F32)

    @pl.when(j == n_f)
    def _():
        y = acc_ref[...]
        scale_ref[...] = lax.rsqrt(jnp.mean(y * y, axis=-1, keepdims=True) + RMS_EPS)

    @pl.when(j >= n_f)
    def _():
        cs = pl.ds(pl.multiple_of((j - n_f) * FIN_TILE, FIN_TILE), FIN_TILE)
        o_ref[...] = x1_ref[...] + acc_ref[:, cs] * scale_ref[...] * g_ref[...]
        acc_ref[:, cs] = jnp.zeros((acc_ref.shape[0], FIN_TILE), F32)


def _ffn(h, w_gate_b, w_up_b, w_down_b, x1, g):
    m, d = h.shape
    d_ff = w_gate_b.shape[1]
    assert d_ff % FF_TILE == 0 and d % FIN_TILE == 0 and d % DOWN_CHUNK == 0
    n_f = d_ff // FF_TILE
    n_fin = d // FIN_TILE
    ftile = lambda i, j: jnp.minimum(j, n_f - 1)
    fin = lambda i, j: jnp.maximum(j - n_f, 0)
    return pl.pallas_call(
        functools.partial(_ffn_kernel, n_f=n_f),
        grid=(m // ROW_TILE, n_f + n_fin),
        in_specs=[pl.BlockSpec((ROW_TILE, d), lambda i, j: (i, 0)),
                  pl.BlockSpec((d, FF_TILE), lambda i, j: (0, ftile(i, j))),
                  pl.BlockSpec((d, FF_TILE), lambda i, j: (0, ftile(i, j))),
                  pl.BlockSpec((FF_TILE, d), lambda i, j: (ftile(i, j), 0)),
                  pl.BlockSpec((ROW_TILE, FIN_TILE), lambda i, j: (i, fin(i, j))),
                  pl.BlockSpec((1, FIN_TILE), lambda i, j: (0, fin(i, j)))],
        out_specs=pl.BlockSpec((ROW_TILE, FIN_TILE), lambda i, j: (i, fin(i, j))),
        out_shape=jax.ShapeDtypeStruct((m, d), F32),
        scratch_shapes=[pltpu.VMEM((ROW_TILE, d), F32), pltpu.VMEM((ROW_TILE, 1), F32)],
        compiler_params=_params("arbitrary", "arbitrary"),
        name="ffn",
    )(h, w_gate_b, w_up_b, w_down_b, x1, g)


def kernel(x, positions, mix_pre_g, w_in, sinks, conv_w, conv_b, conv_ln_g, conv_ln_b, attn_out_g,
           conv_out_g, w_out, mix_post_g, ffn_pre_g, w_gate, w_up, w_down, ffn_post_g):
    b, s, d = x.shape
    depth = w_in.shape[0]
    attn_width = attn_out_g.shape[1]
    conv_width = conv_out_g.shape[1]
    kv_width = (w_in.shape[2] - attn_width - 2 * conv_width) // 2
    m = b * s
    assert m % ROW_TILE == 0 and m % NORM_ROWS == 0

    x2 = x.reshape(m, d)
    pos2 = positions.reshape(m, 1)
    row = lambda v: v.reshape(1, -1)

    for l in range(depth):
        w_in_b = w_in[l].astype(BF16)

        h1, *rope_tabs = _prenorm(x2, row(mix_pre_g[l]), pos2)
        qkv, w_down_b = _qkv_proj(h1, w_in_b, rope_tabs, attn_width, kv_width, w_down[l])
        qkv_n = attn_width + 2 * kv_width
        attn_n, u, w_gate_b, w_out_b = _attn_glu(
            qkv.reshape(b, s, qkv_n), sinks[l], row(attn_out_g[l]), attn_width, kv_width,
            h1, w_in_b, qkv_n, qkv_n + conv_width, conv_width, w_gate[l], w_out[l])
        y, w_up_b = _conv_outproj(attn_n.reshape(m, attn_width), u.reshape(b, s, conv_width), conv_w[l],
                                  row(conv_b[l]), row(conv_ln_g[l]), row(conv_ln_b[l]), row(conv_out_g[l]),
                                  w_out_b, w_up[l])
        x1, h2 = _postmix(y, x2, row(mix_post_g[l]), row(ffn_pre_g[l]))

        x2 = _ffn(h2, w_gate_b, w_up_b, w_down_b, x1, row(ffn_post_g[l]))
    return x2.reshape(b, s, d)
```
